```python
import jax, jax.numpy as jnp
from jax import lax
import numpy as np

D_MODEL = 1024
BATCH = 8
SEQ = 8192
DEPTH = 4
DEC_BATCH = 16
DEC_SEQ = 2048
PAST_LEN = 128

N_MEM = 256
POOL_WINDOWS = (2, 4, 8, 16)
POOL_GROUPS = len(POOL_WINDOWS)
POOL_WIDTH = D_MODEL // 2
POOL_GROUP_W = POOL_WIDTH // POOL_GROUPS
MLA_HEADS = 8
QK_NOPE = 64
QK_ROPE = 32
V_HEAD = 64
Q_LORA = 384
KV_LORA = 256
MLA_WIDTH = MLA_HEADS * V_HEAD
ROPE_THETA = 10000.0
Q_BLOCK = 128
X_HEADS = 4
X_HEAD_DIM = 128
X_WIDTH = X_HEADS * X_HEAD_DIM
N_BRANCH = 3
BRANCH_WIDTH = 512
IN_SPLITS = (POOL_WIDTH, Q_LORA, KV_LORA, QK_ROPE, X_WIDTH)
IN_WIDTH = sum(IN_SPLITS)
IN_OFFSETS = tuple(int(v) for v in np.cumsum(IN_SPLITS)[:-1])
D_FF = -(-8 * D_MODEL // (3 * 256)) * 256
EPS = 1e-6

kernel_name = 'hybrid_pool_mla_memory_encoder'


def rmsnorm(x, g):
    xf = x.astype(jnp.float32)
    y = xf * lax.rsqrt(jnp.mean(xf * xf, axis=-1, keepdims=True) + EPS)
    return (y * g.astype(jnp.float32)).astype(x.dtype)


def rope_tables(seq, dtype):
    inv = 1.0 / (ROPE_THETA ** (np.arange(0, QK_ROPE, 2, dtype=np.float32) / QK_ROPE))
    ang = np.arange(seq, dtype=np.float32)[:, None] * inv[None, :]
    return jnp.asarray(np.cos(ang), dtype), jnp.asarray(np.sin(ang), dtype)


def apply_rope(x, cos, sin):
    half = x.shape[-1] // 2
    x1, x2 = x[..., :half], x[..., half:]
    return jnp.concatenate([x1 * cos - x2 * sin, x2 * cos + x1 * sin], axis=-1)


def multiscale_pool(u, mix, scale):
    B, S, _ = u.shape
    ug = u.reshape(B, S, POOL_GROUPS, POOL_GROUP_W)
    uf = ug.astype(jnp.float32)
    c = jnp.pad(jnp.cumsum(uf, axis=1), ((0, 0), (1, 0), (0, 0), (0, 0)))
    t = np.arange(S)
    pooled = []
    for gi, w in enumerate(POOL_WINDOWS):
        lo = np.clip(t - w // 2, 0, S)
        hi = np.clip(t + w // 2, 0, S)
        cnt = (hi - lo).astype(np.float32)[None, :, None]
        cg = c[:, :, gi]
        pooled.append((cg[:, hi] - cg[:, lo]) / cnt)
    pooled = jnp.stack(pooled, axis=2)
    diff = (pooled - uf).astype(u.dtype)
    y = jnp.einsum('bsgc,gcd->bsgd', diff, mix).reshape(B, S, POOL_WIDTH)
    return y * scale


def latent_attention(cq, ckv, kr, q_norm, kv_norm, w_uq, w_uk, w_uv, cos, sin):
    B, S, _ = cq.shape
    q = (rmsnorm(cq, q_norm) @ w_uq).reshape(B, S, MLA_HEADS, QK_NOPE + QK_ROPE)
    qn = q[..., :QK_NOPE]
    qr = apply_rope(q[..., QK_NOPE:], cos[:, None, :], sin[:, None, :])
    c = rmsnorm(ckv, kv_norm)
    kn = (c @ w_uk).reshape(B, S, MLA_HEADS, QK_NOPE)
    v = (c @ w_uv).reshape(B, S, MLA_HEADS, V_HEAD)
    kr = apply_rope(kr, cos, sin)
    scale = (QK_NOPE + QK_ROPE) ** -0.5
    nb = S // Q_BLOCK
    qn_b = qn.reshape(B, nb, Q_BLOCK, MLA_HEADS, QK_NOPE).transpose(1, 0, 2, 3, 4)
    qr_b = qr.reshape(B, nb, Q_BLOCK, MLA_HEADS, QK_ROPE).transpose(1, 0, 2, 3, 4)

    def block(args):
        qnb, qrb = args
        s = jnp.einsum('bqhd,bkhd->bhqk', qnb, kn) + jnp.einsum('bqhr,bkr->bhqk', qrb, kr)
        p = jax.nn.softmax(s.astype(jnp.float32) * scale, axis=-1).astype(v.dtype)
        return jnp.einsum('bhqk,bkhd->bqhd', p, v)

    o = lax.map(block, (qn_b, qr_b))
    return o.transpose(1, 0, 2, 3, 4).reshape(B, S, MLA_WIDTH)


def memory_attention(qx, mem_n, w_mem_kv):
    B, S, _ = qx.shape
    q = qx.reshape(B, S, X_HEADS, X_HEAD_DIM)
    kv = mem_n @ w_mem_kv
    k = kv[..., :X_WIDTH].reshape(B, -1, X_HEADS, X_HEAD_DIM)
    v = kv[..., X_WIDTH:].reshape(B, -1, X_HEADS, X_HEAD_DIM)
    s = jnp.einsum('bshd,bmhd->bhsm', q, k)
    p = jax.nn.softmax(s.astype(jnp.float32) * (X_HEAD_DIM ** -0.5), axis=-1).astype(v.dtype)
    return jnp.einsum('bhsm,bmhd->bshd', p, v).reshape(B, S, X_WIDTH)


def encoder_layer(x, mem, cos, sin, w_in, q_norm, kv_norm, w_uq, w_uk, w_uv, pool_mix, pool_scale,
                  mem_norm, w_mem_kv, w_branch, w_gate, b_gate, w_out,
                  ln_mix_pre, ln_mix_post, ln_ffn_pre, ln_ffn_post, w_gu, w_down):
    B, S, D = x.shape
    h = rmsnorm(x, ln_mix_pre)
    z = h @ w_in
    u_pool, cq, ckv, kr, qx = jnp.split(z, IN_OFFSETS, axis=-1)
    a_out = multiscale_pool(u_pool, pool_mix, pool_scale)
    b_out = latent_attention(cq, ckv, kr, q_norm, kv_norm, w_uq, w_uk, w_uv, cos, sin)
    m_out = memory_attention(qx, rmsnorm(mem, mem_norm), w_mem_kv)
    br = jnp.stack([a_out, b_out, m_out], axis=2)
    br = jnp.einsum('bsnc,ncd->bsnd', br, w_branch)
    g = jax.nn.sigmoid(h @ w_gate + b_gate).reshape(B, S, N_BRANCH, D)
    merged = jnp.sum(g * br, axis=2)
    x = x + rmsnorm(merged @ w_out, ln_mix_post)
    h = rmsnorm(x, ln_ffn_pre)
    gu = h @ w_gu
    f = (jax.nn.silu(gu[..., :D_FF]) * gu[..., D_FF:]) @ w_down
    return x + rmsnorm(f, ln_ffn_post)


def setup_inputs(seed: int = 0) -> dict:
    key = jax.random.key(seed)
    ks = jax.random.split(key, 26)

    def nrm(k, shape, scale):
        return jax.random.normal(k, shape, jnp.float32) * scale

    def gain(k, shape):
        return 1.0 + 0.02 * jax.random.normal(k, shape, jnp.float32)

    L, D = DEPTH, D_MODEL
    return {
        'x_prompt': nrm(ks[0], (BATCH, SEQ, D), 1.0),
        'x_sample': nrm(ks[1], (DEC_BATCH, DEC_SEQ, D), 1.0),
        'mem_prompt': nrm(ks[2], (BATCH, N_MEM, D), 1.0),
        'mem_sample': nrm(ks[3], (DEC_BATCH, N_MEM, D), 1.0),
        'w_in': nrm(ks[4], (L, D, IN_WIDTH), D ** -0.5),
        'q_norm': gain(ks[5], (L, Q_LORA)),
        'kv_norm': gain(ks[6], (L, KV_LORA)),
        'w_uq': nrm(ks[7], (L, Q_LORA, MLA_HEADS * (QK_NOPE + QK_ROPE)), Q_LORA ** -0.5),
        'w_uk': nrm(ks[8], (L, KV_LORA, MLA_HEADS * QK_NOPE), KV_LORA ** -0.5),
        'w_uv': nrm(ks[9], (L, KV_LORA, MLA_HEADS * V_HEAD), KV_LORA ** -0.5),
        'pool_mix': nrm(ks[10], (L, POOL_GROUPS, POOL_GROUP_W, POOL_GROUP_W), POOL_GROUP_W ** -0.5),
        'pool_scale': gain(ks[11], (L, POOL_WIDTH)),
        'mem_norm': gain(ks[12], (L, D)),
        'w_mem_kv': nrm(ks[13], (L, D, 2 * X_WIDTH), D ** -0.5),
        'w_branch': nrm(ks[14], (L, N_BRANCH, BRANCH_WIDTH, D), BRANCH_WIDTH ** -0.5),
        'w_gate': nrm(ks[15], (L, D, N_BRANCH * D), D ** -0.5),
        'b_gate': nrm(ks[16], (L, N_BRANCH * D), 0.02),
        'w_out': nrm(ks[17], (L, D, D), D ** -0.5),
        'ln_mix_pre': gain(ks[18], (L, D)),
        'ln_mix_post': gain(ks[19], (L, D)),
        'ln_ffn_pre': gain(ks[20], (L, D)),
        'ln_ffn_post': gain(ks[21], (L, D)),
        'w_gu': nrm(ks[22], (L, D, 2 * D_FF), D ** -0.5),
        'w_down': nrm(ks[23], (L, D_FF, D), D_FF ** -0.5),
    }


def reference(x_prompt, x_sample, mem_prompt, mem_sample, w_in, q_norm, kv_norm, w_uq, w_uk, w_uv,
              pool_mix, pool_scale, mem_norm, w_mem_kv, w_branch, w_gate, b_gate, w_out,
              ln_mix_pre, ln_mix_post, ln_ffn_pre, ln_ffn_post, w_gu, w_down):
    def trunk(x, mem):
        cos, sin = rope_tables(x.shape[1], x.dtype)
        for l in range(DEPTH):
            x = encoder_layer(x, mem, cos, sin, w_in[l], q_norm[l], kv_norm[l], w_uq[l], w_uk[l], w_uv[l],
                              pool_mix[l], pool_scale[l], mem_norm[l], w_mem_kv[l], w_branch[l],
                              w_gate[l], b_gate[l], w_out[l], ln_mix_pre[l], ln_mix_post[l],
                              ln_ffn_pre[l], ln_ffn_post[l], w_gu[l], w_down[l])
        return x

    y_prompt = trunk(x_prompt, mem_prompt)
    y_sample = trunk(x_sample, mem_sample)
    return (y_prompt, y_sample)
```

```python
import functools

import jax
import jax.numpy as jnp
import numpy as np
from jax import lax
from jax.experimental import pallas as pl
from jax.experimental.pallas import tpu as pltpu

D_MODEL = 1024
DEPTH = 4
N_MEM = 256
POOL_WINDOWS = (2, 4, 8, 16)
POOL_GROUPS = len(POOL_WINDOWS)
POOL_WIDTH = 512
POOL_GROUP_W = POOL_WIDTH // POOL_GROUPS
MLA_HEADS = 8
QK_NOPE = 64
QK_ROPE = 32
ROPE_HALF = QK_ROPE // 2
V_HEAD = 64
Q_LORA = 384
KV_LORA = 256
MLA_WIDTH = MLA_HEADS * V_HEAD
ROPE_THETA = 10000.0
X_HEADS = 4
X_HEAD_DIM = 128
X_WIDTH = X_HEADS * X_HEAD_DIM
N_BRANCH = 3
BRANCH_WIDTH = 512
D_FF = 2816
EPS = 1e-6

LANES = 128
HEAD_PAD = LANES
HALO = 16
Z_POOL = 0
Z_CQ = Z_POOL + POOL_WIDTH
Z_CKV = Z_CQ + Q_LORA
Z_QX = Z_CKV + KV_LORA
Z_KR = Z_QX + X_WIDTH
Z_WIDTH = Z_KR + LANES
FF_CHUNK = 256
LOG2E = 1.4426950408889634
VMEM_LIMIT = 56 * 1024 * 1024

TM_PRE = 512
TM_MERGE = 512
TM_FFN = 512
TQ_ATT = 512
TK_ATT = 512

BF = jnp.bfloat16
F32 = jnp.float32


def _rms(x, g):
    y = x * lax.rsqrt(jnp.mean(x * x, axis=-1, keepdims=True) + EPS)
    return y * g


def _const_spec(shape):
    nd = len(shape)
    return pl.BlockSpec(shape, lambda *_: (0,) * nd, pipeline_mode=pl.Buffered(1))


def _params():
    return pltpu.CompilerParams(vmem_limit_bytes=VMEM_LIMIT)


def _memkv_kernel(mem_ref, g_ref, w_ref, k_ref, v_ref):
    mn = _rms(mem_ref[0], g_ref[0]).astype(BF)
    kv = jnp.dot(mn, w_ref[0], preferred_element_type=F32)
    k_ref[0, 0] = kv[:, :X_WIDTH].astype(BF)
    v_ref[0, 0] = kv[:, X_WIDTH:].astype(BF)


def _memkv(mem, mem_norm, w_mem_kv):
    B = mem.shape[0]
    out = jax.ShapeDtypeStruct((DEPTH, B, N_MEM, X_WIDTH), BF)
    return pl.pallas_call(
        _memkv_kernel,
        grid=(DEPTH, B),
        in_specs=[
            pl.BlockSpec((1, N_MEM, D_MODEL), lambda l, b: (b, 0, 0)),
            pl.BlockSpec((1, 1, D_MODEL), lambda l, b: (l, 0, 0)),
            pl.BlockSpec((1, D_MODEL, 2 * X_WIDTH), lambda l, b: (l, 0, 0)),
        ],
        out_specs=[
            pl.BlockSpec((1, 1, N_MEM, X_WIDTH), lambda l, b: (l, b, 0, 0)),
            pl.BlockSpec((1, 1, N_MEM, X_WIDTH), lambda l, b: (l, b, 0, 0)),
        ],
        out_shape=[out, out],
        name="memkv",
        compiler_params=_params(),
    )(mem, mem_norm.reshape(DEPTH, 1, D_MODEL), w_mem_kv)


def _rope_slab(x, ct, sa, sb):
    return x * ct + pltpu.roll(x, ROPE_HALF, 1) * sa + pltpu.roll(x, LANES - ROPE_HALF, 1) * sb


def _pre_kernel(x_ref, ct_ref, sa_ref, sb_ref, kmem_ref, vmem_ref, ln_ref, w_in_ref, qn_ref, kvn_ref,
                w_uq_ref, w_uk_ref, w_uv_ref, vone_ref,
                u_ref, q_ref, k_ref, v_ref, m_ref):
    h = _rms(x_ref[0], ln_ref[...]).astype(BF)
    z = jnp.dot(h, w_in_ref[...], preferred_element_type=F32)
    u_ref[0] = z[:, Z_POOL:Z_POOL + POOL_WIDTH]

    ct, sa, sb = ct_ref[...], sa_ref[...], sb_ref[...]
    q_scale = (QK_NOPE + QK_ROPE) ** -0.5 * LOG2E
    cq = _rms(z[:, Z_CQ:Z_CQ + Q_LORA], qn_ref[...]).astype(BF)
    qf = jnp.dot(cq, w_uq_ref[...], preferred_element_type=F32)
    c = _rms(z[:, Z_CKV:Z_CKV + KV_LORA], kvn_ref[...]).astype(BF)
    kf = jnp.dot(c, w_uk_ref[...], preferred_element_type=F32)
    vf = jnp.dot(c, w_uv_ref[...], preferred_element_type=F32) + vone_ref[...]
    kr = _rope_slab(z[:, Z_KR:Z_KR + LANES], ct, sa, sb)
    for hd in range(MLA_HEADS):
        sl = slice(hd * HEAD_PAD, (hd + 1) * HEAD_PAD)
        q_ref[0, hd] = (_rope_slab(qf[:, sl], ct, sa, sb) * q_scale).astype(BF)
        k_ref[0, hd] = (kf[:, sl] + kr).astype(BF)
        v_ref[0, hd] = vf[:, sl].astype(BF)

    x_scale = X_HEAD_DIM ** -0.5 * LOG2E
    qx = (z[:, Z_QX:Z_QX + X_WIDTH] * x_scale).astype(BF)
    outs = []
    for hd in range(X_HEADS):
        sl = slice(hd * X_HEAD_DIM, (hd + 1) * X_HEAD_DIM)
        s = lax.dot_general(qx[:, sl], kmem_ref[0, 0, :, sl], (((1,), (1,)), ((), ())),
                            preferred_element_type=F32)
        p = jnp.exp2(s - jnp.max(s, axis=-1, keepdims=True))
        l = jnp.sum(p, axis=-1, keepdims=True)
        o = jnp.dot(p.astype(BF), vmem_ref[0, 0, :, sl], preferred_element_type=F32)
        outs.append(o / l)
    m_ref[0] = jnp.concatenate(outs, axis=-1).astype(BF)


def _pre(x, tabs, kmem, vmem, l, ln, w_in, qn, kvn, w_uq, w_uk, w_uv, vone):
    B, S, _ = x.shape
    tm = TM_PRE
    row = lambda b, i: (b, i, 0)
    head = lambda b, i: (b, 0, i, 0)
    tab = lambda b, i: (i, 0)
    att = jax.ShapeDtypeStruct((B, MLA_HEADS, S, HEAD_PAD), BF)
    return pl.pallas_call(
        _pre_kernel,
        grid=(B, S // tm),
        in_specs=[
            pl.BlockSpec((1, tm, D_MODEL), row),
            pl.BlockSpec((tm, LANES), tab), pl.BlockSpec((tm, LANES), tab), pl.BlockSpec((tm, LANES), tab),
            pl.BlockSpec((1, 1, N_MEM, X_WIDTH), lambda b, i: (l, b, 0, 0)),
            pl.BlockSpec((1, 1, N_MEM, X_WIDTH), lambda b, i: (l, b, 0, 0)),
            _const_spec((1, D_MODEL)), _const_spec((D_MODEL, Z_WIDTH)),
            _const_spec((1, Q_LORA)), _const_spec((1, KV_LORA)),
            _const_spec((Q_LORA, MLA_HEADS * HEAD_PAD)),
            _const_spec((KV_LORA, MLA_HEADS * HEAD_PAD)),
            _const_spec((KV_LORA, MLA_HEADS * HEAD_PAD)),
            _const_spec((1, MLA_HEADS * HEAD_PAD)),
        ],
        out_specs=[
            pl.BlockSpec((1, tm, POOL_WIDTH), row),
            pl.BlockSpec((1, MLA_HEADS, tm, HEAD_PAD), head),
            pl.BlockSpec((1, MLA_HEADS, tm, HEAD_PAD), head),
            pl.BlockSpec((1, MLA_HEADS, tm, HEAD_PAD), head),
            pl.BlockSpec((1, tm, X_WIDTH), row),
        ],
        out_shape=[
            jax.ShapeDtypeStruct((B, S, POOL_WIDTH), F32), att, att, att,
            jax.ShapeDtypeStruct((B, S, X_WIDTH), BF),
        ],
        name="pre",
        compiler_params=_params(),
    )(x, *tabs, kmem, vmem, ln, w_in, qn, kvn, w_uq, w_uk, w_uv, vone)


def _attn_kernel(q_ref, k_ref, v_ref, o_ref, *, n_chunks, tk):
    tq = q_ref.shape[2]
    qs = [q_ref[0, j] for j in range(2)]

    def body(c, carry):
        new = []
        for j in range(2):
            m, acc = carry[j]
            off = pl.multiple_of(c * tk, tk)
            k = k_ref[0, j, pl.ds(off, tk), :]
            v = v_ref[0, j, pl.ds(off, tk), :]
            s = lax.dot_general(qs[j], k, (((1,), (1,)), ((), ())), preferred_element_type=F32)
            m_new = jnp.maximum(m, jnp.max(s, axis=-1, keepdims=True))
            alpha = jnp.exp2(m - m_new)
            p = jnp.exp2(s - m_new).astype(BF)
            acc = acc * alpha + jnp.dot(p, v, preferred_element_type=F32)
            new.append((m_new, acc))
        return tuple(new)

    init = tuple((jnp.full((tq, 1), -jnp.inf, F32), jnp.zeros((tq, HEAD_PAD), F32)) for _ in range(2))
    (_, acc0), (_, acc1) = lax.fori_loop(0, n_chunks, body, init)
    o0 = acc0 / acc0[:, V_HEAD:V_HEAD + 1]
    o1 = acc1 / acc1[:, 0:1]
    lane = lax.broadcasted_iota(jnp.int32, (tq, HEAD_PAD), 1)
    o_ref[0] = jnp.where(lane < V_HEAD, o0, o1).astype(BF)


def _attn(q, k, v):
    B, H, S, _ = q.shape
    tq, tk = TQ_ATT, TK_ATT
    kern = functools.partial(_attn_kernel, n_chunks=S // tk, tk=tk)
    return pl.pallas_call(
        kern,
        grid=(B, H // 2, S // tq),
        in_specs=[
            pl.BlockSpec((1, 2, tq, HEAD_PAD), lambda b, hp, i: (b, hp, i, 0)),
            pl.BlockSpec((1, 2, S, HEAD_PAD), lambda b, hp, i: (b, hp, 0, 0)),
            pl.BlockSpec((1, 2, S, HEAD_PAD), lambda b, hp, i: (b, hp, 0, 0)),
        ],
        out_specs=pl.BlockSpec((1, tq, 2 * V_HEAD), lambda b, hp, i: (b, i, hp)),
        out_shape=jax.ShapeDtypeStruct((B, S, MLA_WIDTH), BF),
        name="attn",
        compiler_params=_params(),
    )(q, k, v)


def _merge_kernel(x_ref, u_ref, up_ref, un_ref, b_ref, m_ref, ln_ref, wg_ref, bg_ref, mix_ref, ps_ref,
                  wbr_ref, wo_ref, lnp_ref, o_ref, ubuf, *, seq):
    tm = x_ref.shape[1]
    i = pl.program_id(1)
    x = x_ref[0]
    h = _rms(x, ln_ref[...]).astype(BF)

    ubuf[0:HALO, :] = jnp.where(i > 0, up_ref[0], 0.0)
    ubuf[HALO:HALO + tm, :] = u_ref[0]
    ubuf[HALO + tm:, :] = jnp.where(i < pl.num_programs(1) - 1, un_ref[0], 0.0)
    t = i * tm + lax.broadcasted_iota(jnp.int32, (tm, 1), 0)
    a_parts = []
    for g, w in enumerate(POOL_WINDOWS):
        sl = slice(g * POOL_GROUP_W, (g + 1) * POOL_GROUP_W)
        tot = ubuf[HALO - w // 2:HALO - w // 2 + tm, sl]
        for j in range(1 - w // 2, w // 2):
            tot = tot + ubuf[HALO + j:HALO + j + tm, sl]
        cnt = (jnp.minimum(t + w // 2, seq) - jnp.maximum(t - w // 2, 0)).astype(F32)
        diff = (tot / cnt - ubuf[HALO:HALO + tm, sl]).astype(BF)
        a_parts.append(jnp.dot(diff, mix_ref[g], preferred_element_type=F32))
    a_out = (jnp.concatenate(a_parts, axis=-1) * ps_ref[...]).astype(BF)

    branches = (a_out, b_ref[0], m_ref[0])
    merged = None
    for n in range(N_BRANCH):
        sl = slice(n * D_MODEL, (n + 1) * D_MODEL)
        gate = jax.nn.sigmoid(jnp.dot(h, wg_ref[:, sl], preferred_element_type=F32) + bg_ref[:, sl])
        term = gate * jnp.dot(branches[n], wbr_ref[n], preferred_element_type=F32)
        merged = term if merged is None else merged + term
    y = jnp.dot(merged.astype(BF), wo_ref[...], preferred_element_type=F32)
    o_ref[0] = x + _rms(y, lnp_ref[...])


def _merge(x, u, b_out, m_out, ln, wg, bg, mix, ps, wbr, wo, lnp):
    B, S, _ = x.shape
    tm = TM_MERGE
    hb = tm // HALO
    n_halo = S // HALO
    row = lambda b, i: (b, i, 0)
    return pl.pallas_call(
        functools.partial(_merge_kernel, seq=S),
        grid=(B, S // tm),
        in_specs=[
            pl.BlockSpec((1, tm, D_MODEL), row),
            pl.BlockSpec((1, tm, POOL_WIDTH), row),
            pl.BlockSpec((1, HALO, POOL_WIDTH), lambda b, i: (b, jnp.maximum(i * hb - 1, 0), 0)),
            pl.BlockSpec((1, HALO, POOL_WIDTH), lambda b, i: (b, jnp.minimum((i + 1) * hb, n_halo - 1), 0)),
            pl.BlockSpec((1, tm, MLA_WIDTH), row),
            pl.BlockSpec((1, tm, X_WIDTH), row),
            _const_spec((1, D_MODEL)),
            _const_spec((D_MODEL, N_BRANCH * D_MODEL)),
            _const_spec((1, N_BRANCH * D_MODEL)),
            _const_spec((POOL_GROUPS, POOL_GROUP_W, POOL_GROUP_W)),
            _const_spec((1, POOL_WIDTH)),
            _const_spec((N_BRANCH, BRANCH_WIDTH, D_MODEL)),
            _const_spec((D_MODEL, D_MODEL)),
            _const_spec((1, D_MODEL)),
        ],
        out_specs=pl.BlockSpec((1, tm, D_MODEL), row),
        out_shape=jax.ShapeDtypeStruct((B, S, D_MODEL), F32),
        scratch_shapes=[pltpu.VMEM((tm + 2 * HALO, POOL_WIDTH), F32)],
        name="merge",
        compiler_params=_params(),
    )(x, u, u, u, b_out, m_out, ln, wg, bg, mix, ps, wbr, wo, lnp)


def _ffn_kernel(x_ref, ln_ref, wgu_ref, wd_ref, lnp_ref, o_ref):
    x = x_ref[...]
    h = _rms(x, ln_ref[...]).astype(BF)
    f = None
    for c in range(D_FF // FF_CHUNK):
        g = jnp.dot(h, wgu_ref[:, c * FF_CHUNK:(c + 1) * FF_CHUNK], preferred_element_type=F32)
        u = jnp.dot(h, wgu_ref[:, D_FF + c * FF_CHUNK:D_FF + (c + 1) * FF_CHUNK], preferred_element_type=F32)
        a = (g * jax.nn.sigmoid(g) * u).astype(BF)
        part = jnp.dot(a, wd_ref[c * FF_CHUNK:(c + 1) * FF_CHUNK, :], preferred_element_type=F32)
        f = part if f is None else f + part
    o_ref[...] = x + _rms(f, lnp_ref[...])


def _ffn(x, ln, wgu, wd, lnp):
    B, S, _ = x.shape
    n = B * S
    tm = TM_FFN
    out = pl.pallas_call(
        _ffn_kernel,
        grid=(n // tm,),
        in_specs=[
            pl.BlockSpec((tm, D_MODEL), lambda i: (i, 0)),
            _const_spec((1, D_MODEL)),
            _const_spec((D_MODEL, 2 * D_FF)),
            _const_spec((D_FF, D_MODEL)),
            _const_spec((1, D_MODEL)),
        ],
        out_specs=pl.BlockSpec((tm, D_MODEL), lambda i: (i, 0)),
        out_shape=jax.ShapeDtypeStruct((n, D_MODEL), F32),
        name="ffn",
        compiler_params=_params(),
    )(x.reshape(n, D_MODEL), ln, wgu, wd, lnp)
    return out.reshape(B, S, D_MODEL)


def _rope_tables(seq):
    inv = 1.0 / (ROPE_THETA ** (np.arange(0, QK_ROPE, 2, dtype=np.float32) / QK_ROPE))
    ang = np.arange(seq, dtype=np.float32)[:, None] * inv[None, :]
    cos, sin = np.cos(ang), np.sin(ang)
    ct = np.zeros((seq, LANES), np.float32)
    sa = np.zeros((seq, LANES), np.float32)
    sb = np.zeros((seq, LANES), np.float32)
    x1 = slice(QK_NOPE, QK_NOPE + ROPE_HALF)
    x2 = slice(QK_NOPE + ROPE_HALF, QK_NOPE + QK_ROPE)
    ct[:, :QK_NOPE] = 1.0
    ct[:, x1] = cos
    ct[:, x2] = cos
    sa[:, x2] = sin
    sb[:, x1] = -sin
    return jnp.asarray(ct), jnp.asarray(sa), jnp.asarray(sb)


def _head_slabs(w, width, offset):
    L, K, _ = w.shape
    w = w.reshape(L, K, MLA_HEADS, width)
    w = jnp.pad(w, ((0, 0), (0, 0), (0, 0), (offset, HEAD_PAD - width - offset)))
    return w.reshape(L, K, MLA_HEADS * HEAD_PAD)


def _prep_weights(w_in, w_uq, w_uk, w_uv):
    pool, cq, ckv, kr, qx = (w_in[..., a:b] for a, b in
                             ((0, 512), (512, 896), (896, 1152), (1152, 1184), (1184, 1696)))
    kr_slab = jnp.pad(kr, ((0, 0), (0, 0), (QK_NOPE, LANES - QK_NOPE - QK_ROPE)))
    w_in_p = jnp.concatenate([pool, cq, ckv, qx, kr_slab], axis=-1).astype(BF)
    w_uq_p = _head_slabs(w_uq, QK_NOPE + QK_ROPE, 0).astype(BF)
    w_uk_p = _head_slabs(w_uk, QK_NOPE, 0).astype(BF)
    v_even = _head_slabs(w_uv, V_HEAD, 0).reshape(DEPTH, KV_LORA, MLA_HEADS, HEAD_PAD)
    v_odd = _head_slabs(w_uv, V_HEAD, V_HEAD).reshape(DEPTH, KV_LORA, MLA_HEADS, HEAD_PAD)
    odd = (jnp.arange(MLA_HEADS) % 2 == 1)[None, None, :, None]
    w_uv_p = jnp.where(odd, v_odd, v_even).reshape(DEPTH, KV_LORA, MLA_HEADS * HEAD_PAD).astype(BF)
    vone = np.zeros((MLA_HEADS, HEAD_PAD), np.float32)
    vone[0::2, V_HEAD] = 1.0
    vone[1::2, 0] = 1.0
    return w_in_p, w_uq_p, w_uk_p, w_uv_p, jnp.asarray(vone.reshape(1, -1))


def kernel(x_prompt, x_sample, mem_prompt, mem_sample, w_in, q_norm, kv_norm, w_uq, w_uk, w_uv, pool_mix,
           pool_scale, mem_norm, w_mem_kv, w_branch, w_gate, b_gate, w_out, ln_mix_pre, ln_mix_post,
           ln_ffn_pre, ln_ffn_post, w_gu, w_down):
    w_in_p, w_uq_p, w_uk_p, w_uv_p, vone = _prep_weights(w_in, w_uq, w_uk, w_uv)
    w_mem_kv_b = w_mem_kv.astype(BF)
    w_gate_b, w_branch_b, w_out_b = w_gate.astype(BF), w_branch.astype(BF), w_out.astype(BF)
    pool_mix_b, w_gu_b, w_down_b = pool_mix.astype(BF), w_gu.astype(BF), w_down.astype(BF)
    vec = lambda a, l: a[l][None, :]

    def trunk(x, mem):
        tabs = _rope_tables(x.shape[1])
        kmem, vmem = _memkv(mem, mem_norm, w_mem_kv_b)
        for l in range(DEPTH):
            u, q, k, v, m_out = _pre(x, tabs, kmem, vmem, l, vec(ln_mix_pre, l), w_in_p[l], vec(q_norm, l),
                                     vec(kv_norm, l), w_uq_p[l], w_uk_p[l], w_uv_p[l], vone)
            b_out = _attn(q, k, v)
            x = _merge(x, u, b_out, m_out, vec(ln_mix_pre, l), w_gate_b[l], vec(b_gate, l), pool_mix_b[l],
                       vec(pool_scale, l), w_branch_b[l], w_out_b[l], vec(ln_mix_post, l))
            x = _ffn(x, vec(ln_ffn_pre, l), w_gu_b[l], w_down_b[l], vec(ln_ffn_post, l))
        return x

    return (trunk(x_prompt, mem_prompt), trunk(x_sample, mem_sample))
```

```python
import functools

import jax
import jax.numpy as jnp
import numpy as np
from jax import lax
from jax.experimental import pallas as pl
from jax.experimental.pallas import tpu as pltpu

D_MODEL = 1024
DEPTH = 4
N_MEM = 256
POOL_WINDOWS = (2, 4, 8, 16)
POOL_GROUPS = len(POOL_WINDOWS)
POOL_WIDTH = 512
POOL_GROUP_W = POOL_WIDTH // POOL_GROUPS
MLA_HEADS = 8
QK_NOPE = 64
QK_ROPE = 32
ROPE_HALF = QK_ROPE // 2
V_HEAD = 64
Q_LORA = 384
KV_LORA = 256
MLA_WIDTH = MLA_HEADS * V_HEAD
ROPE_THETA = 10000.0
X_HEADS = 4
X_HEAD_DIM = 128
X_WIDTH = X_HEADS * X_HEAD_DIM
N_BRANCH = 3
BRANCH_WIDTH = 512
D_FF = 2816
EPS = 1e-6

LANES = 128
HEAD_PAD = LANES
HALO = 16
Z_POOL = 0
Z_CQ = Z_POOL + POOL_WIDTH
Z_CKV = Z_CQ + Q_LORA
Z_QX = Z_CKV + KV_LORA
Z_KR = Z_QX + X_WIDTH
Z_WIDTH = Z_KR + LANES
FF_CHUNK = 256
LOG2E = 1.4426950408889634
VMEM_LIMIT = 56 * 1024 * 1024

TM_PRE = 512
TM_MERGE = 512
TM_FFN = 512
TQ_ATT = 512
TK_ATT = 1024
SOFTMAX_ROWS = 32

BF = jnp.bfloat16
F32 = jnp.float32


def _rms(x, g):
    y = x * lax.rsqrt(jnp.mean(x * x, axis=-1, keepdims=True) + EPS)
    return y * g


def _const_spec(shape):
    nd = len(shape)
    return pl.BlockSpec(shape, lambda *_: (0,) * nd, pipeline_mode=pl.Buffered(1))


def _params():
    return pltpu.CompilerParams(vmem_limit_bytes=VMEM_LIMIT)


def _memkv_kernel(mem_ref, g_ref, w_ref, k_ref, v_ref):
    mn = _rms(mem_ref[0], g_ref[0]).astype(BF)
    kv = jnp.dot(mn, w_ref[0], preferred_element_type=F32)
    k_ref[0, 0] = kv[:, :X_WIDTH].astype(BF)
    v_ref[0, 0] = kv[:, X_WIDTH:].astype(BF)


def _memkv(mem, mem_norm, w_mem_kv):
    B = mem.shape[0]
    out = jax.ShapeDtypeStruct((DEPTH, B, N_MEM, X_WIDTH), BF)
    return pl.pallas_call(
        _memkv_kernel,
        grid=(DEPTH, B),
        in_specs=[
            pl.BlockSpec((1, N_MEM, D_MODEL), lambda l, b: (b, 0, 0)),
            pl.BlockSpec((1, 1, D_MODEL), lambda l, b: (l, 0, 0)),
            pl.BlockSpec((1, D_MODEL, 2 * X_WIDTH), lambda l, b: (l, 0, 0)),
        ],
        out_specs=[
            pl.BlockSpec((1, 1, N_MEM, X_WIDTH), lambda l, b: (l, b, 0, 0)),
            pl.BlockSpec((1, 1, N_MEM, X_WIDTH), lambda l, b: (l, b, 0, 0)),
        ],
        out_shape=[out, out],
        name="memkv",
        compiler_params=_params(),
    )(mem, mem_norm.reshape(DEPTH, 1, D_MODEL), w_mem_kv)


def _rope_slab(x, ct, sa, sb):
    return x * ct + pltpu.roll(x, ROPE_HALF, 1) * sa + pltpu.roll(x, LANES - ROPE_HALF, 1) * sb


def _pre_kernel(x_ref, ct_ref, sa_ref, sb_ref, kmem_ref, vmem_ref, ln_ref, w_in_ref, qn_ref, kvn_ref,
                w_uq_ref, w_uk_ref, w_uv_ref, vone_ref,
                u_ref, q_ref, k_ref, v_ref, m_ref):
    h = _rms(x_ref[0], ln_ref[...]).astype(BF)
    z = jnp.dot(h, w_in_ref[...], preferred_element_type=F32)
    u_ref[0] = z[:, Z_POOL:Z_POOL + POOL_WIDTH]

    ct, sa, sb = ct_ref[...], sa_ref[...], sb_ref[...]
    q_scale = (QK_NOPE + QK_ROPE) ** -0.5 * LOG2E
    cq = _rms(z[:, Z_CQ:Z_CQ + Q_LORA], qn_ref[...]).astype(BF)
    qf = jnp.dot(cq, w_uq_ref[...], preferred_element_type=F32)
    c = _rms(z[:, Z_CKV:Z_CKV + KV_LORA], kvn_ref[...]).astype(BF)
    kf = jnp.dot(c, w_uk_ref[...], preferred_element_type=F32)
    vf = jnp.dot(c, w_uv_ref[...], preferred_element_type=F32) + vone_ref[...]
    kr = _rope_slab(z[:, Z_KR:Z_KR + LANES], ct, sa, sb)
    for hd in range(MLA_HEADS):
        sl = slice(hd * HEAD_PAD, (hd + 1) * HEAD_PAD)
        q_ref[0, hd] = (_rope_slab(qf[:, sl], ct, sa, sb) * q_scale).astype(BF)
        k_ref[0, hd] = (kf[:, sl] + kr).astype(BF)
        v_ref[0, hd] = vf[:, sl].astype(BF)

    x_scale = X_HEAD_DIM ** -0.5 * LOG2E
    qx = (z[:, Z_QX:Z_QX + X_WIDTH] * x_scale).astype(BF)
    outs = []
    for hd in range(X_HEADS):
        sl = slice(hd * X_HEAD_DIM, (hd + 1) * X_HEAD_DIM)
        s = lax.dot_general(qx[:, sl], kmem_ref[0, 0, :, sl], (((1,), (1,)), ((), ())),
                            preferred_element_type=F32)
        p = jnp.exp2(s - jnp.max(s, axis=-1, keepdims=True))
        l = jnp.sum(p, axis=-1, keepdims=True)
        o = jnp.dot(p.astype(BF), vmem_ref[0, 0, :, sl], preferred_element_type=F32)
        outs.append(o / l)
    m_ref[0] = jnp.concatenate(outs, axis=-1).astype(BF)


def _pre(x, tabs, kmem, vmem, l, ln, w_in, qn, kvn, w_uq, w_uk, w_uv, vone):
    B, S, _ = x.shape
    tm = TM_PRE
    row = lambda b, i: (b, i, 0)
    head = lambda b, i: (b, 0, i, 0)
    tab = lambda b, i: (i, 0)
    att = jax.ShapeDtypeStruct((B, MLA_HEADS, S, HEAD_PAD), BF)
    return pl.pallas_call(
        _pre_kernel,
        grid=(B, S // tm),
        in_specs=[
            pl.BlockSpec((1, tm, D_MODEL), row),
            pl.BlockSpec((tm, LANES), tab), pl.BlockSpec((tm, LANES), tab), pl.BlockSpec((tm, LANES), tab),
            pl.BlockSpec((1, 1, N_MEM, X_WIDTH), lambda b, i: (l, b, 0, 0)),
            pl.BlockSpec((1, 1, N_MEM, X_WIDTH), lambda b, i: (l, b, 0, 0)),
            _const_spec((1, D_MODEL)), _const_spec((D_MODEL, Z_WIDTH)),
            _const_spec((1, Q_LORA)), _const_spec((1, KV_LORA)),
            _const_spec((Q_LORA, MLA_HEADS * HEAD_PAD)),
            _const_spec((KV_LORA, MLA_HEADS * HEAD_PAD)),
            _const_spec((KV_LORA, MLA_HEADS * HEAD_PAD)),
            _const_spec((1, MLA_HEADS * HEAD_PAD)),
        ],
        out_specs=[
            pl.BlockSpec((1, tm, POOL_WIDTH), row),
            pl.BlockSpec((1, MLA_HEADS, tm, HEAD_PAD), head),
            pl.BlockSpec((1, MLA_HEADS, tm, HEAD_PAD), head),
            pl.BlockSpec((1, MLA_HEADS, tm, HEAD_PAD), head),
            pl.BlockSpec((1, tm, X_WIDTH), row),
        ],
        out_shape=[
            jax.ShapeDtypeStruct((B, S, POOL_WIDTH), F32), att, att, att,
            jax.ShapeDtypeStruct((B, S, X_WIDTH), BF),
        ],
        name="pre",
        compiler_params=_params(),
    )(x, *tabs, kmem, vmem, ln, w_in, qn, kvn, w_uq, w_uk, w_uv, vone)


def _attn_kernel(q_ref, k_ref, v_ref, o_ref, s_scr, p_scr, m_scr, a_scr, acc_scr, *, n_chunks, tk):
    tq = q_ref.shape[2]
    heads = range(q_ref.shape[1])
    nt = (((1,), (1,)), ((), ()))

    def scores(j, c, slot):
        off = pl.multiple_of(c * tk, tk)
        s_scr[j, slot] = lax.dot_general(q_ref[0, j], k_ref[0, j, pl.ds(off, tk), :], nt,
                                         preferred_element_type=F32)

    def softmax(j, slot):
        for r in range(tq // SOFTMAX_ROWS):
            rows = slice(r * SOFTMAX_ROWS, (r + 1) * SOFTMAX_ROWS)
            s = s_scr[j, slot, rows, :]
            m_old = m_scr[j, rows, :]
            m_new = jnp.maximum(m_old, jnp.max(s, axis=-1, keepdims=True))
            p_scr[j, slot, rows, :] = jnp.exp2(s - jnp.concatenate([m_new] * (tk // LANES), axis=1)).astype(BF)
            m_scr[j, rows, :] = m_new
            a_scr[j, slot, rows, :] = jnp.exp2(m_old - m_new)

    def values(j, c, slot):
        off = pl.multiple_of(c * tk, tk)
        acc_scr[j] = acc_scr[j] * a_scr[j, slot] + jnp.dot(p_scr[j, slot], v_ref[0, j, pl.ds(off, tk), :],
                                                           preferred_element_type=F32)

    for j in heads:
        m_scr[j] = jnp.full((tq, LANES), -jnp.inf, F32)
        acc_scr[j] = jnp.zeros((tq, HEAD_PAD), F32)
        p_scr[j, 1] = jnp.zeros((tq, tk), BF)
        a_scr[j, 1] = jnp.ones((tq, LANES), F32)
        scores(j, 0, 0)

    def chunk_pair(c0, last):
        for j in heads:
            scores(j, c0 + 1, 1)
            softmax(j, 0)
            values(j, jnp.maximum(c0 - 1, 0), 1)
        for j in heads:
            if not last:
                scores(j, c0 + 2, 0)
            softmax(j, 1)
            values(j, c0, 0)

    def body(i, _):
        chunk_pair(2 * i, False)
        return 0

    lax.fori_loop(0, n_chunks // 2 - 1, body, 0)
    chunk_pair(n_chunks - 2, True)
    for j in heads:
        values(j, n_chunks - 1, 1)
    o0 = acc_scr[0] / acc_scr[0][:, V_HEAD:V_HEAD + 1]
    o1 = acc_scr[1] / acc_scr[1][:, 0:1]
    lane = lax.broadcasted_iota(jnp.int32, (tq, HEAD_PAD), 1)
    o_ref[0] = jnp.where(lane < V_HEAD, o0, o1).astype(BF)


def _attn(q, k, v):
    B, H, S, _ = q.shape
    tq, tk = TQ_ATT, TK_ATT
    assert S % (2 * tk) == 0 and S % tq == 0
    kern = functools.partial(_attn_kernel, n_chunks=S // tk, tk=tk)
    return pl.pallas_call(
        kern,
        grid=(B, H // 2, S // tq),
        in_specs=[
            pl.BlockSpec((1, 2, tq, HEAD_PAD), lambda b, hp, i: (b, hp, i, 0)),
            pl.BlockSpec((1, 2, S, HEAD_PAD), lambda b, hp, i: (b, hp, 0, 0)),
            pl.BlockSpec((1, 2, S, HEAD_PAD), lambda b, hp, i: (b, hp, 0, 0)),
        ],
        out_specs=pl.BlockSpec((1, tq, 2 * V_HEAD), lambda b, hp, i: (b, i, hp)),
        out_shape=jax.ShapeDtypeStruct((B, S, MLA_WIDTH), BF),
        scratch_shapes=[
            pltpu.VMEM((2, 2, tq, tk), F32),
            pltpu.VMEM((2, 2, tq, tk), BF),
            pltpu.VMEM((2, tq, LANES), F32),
            pltpu.VMEM((2, 2, tq, LANES), F32),
            pltpu.VMEM((2, tq, HEAD_PAD), F32),
        ],
        name="attn",
        compiler_params=_params(),
    )(q, k, v)


def _merge_kernel(x_ref, u_ref, up_ref, un_ref, b_ref, m_ref, ln_ref, wg_ref, bg_ref, mix_ref, ps_ref,
                  wbr_ref, wo_ref, lnp_ref, o_ref, ubuf, *, seq):
    tm = x_ref.shape[1]
    i = pl.program_id(1)
    x = x_ref[0]
    h = _rms(x, ln_ref[...]).astype(BF)

    ubuf[0:HALO, :] = jnp.where(i > 0, up_ref[0], 0.0)
    ubuf[HALO:HALO + tm, :] = u_ref[0]
    ubuf[HALO + tm:, :] = jnp.where(i < pl.num_programs(1) - 1, un_ref[0], 0.0)
    t = i * tm + lax.broadcasted_iota(jnp.int32, (tm, 1), 0)
    a_parts = []
    for g, w in enumerate(POOL_WINDOWS):
        sl = slice(g * POOL_GROUP_W, (g + 1) * POOL_GROUP_W)
        tot = ubuf[HALO - w // 2:HALO - w // 2 + tm, sl]
        for j in range(1 - w // 2, w // 2):
            tot = tot + ubuf[HALO + j:HALO + j + tm, sl]
        cnt = (jnp.minimum(t + w // 2, seq) - jnp.maximum(t - w // 2, 0)).astype(F32)
        diff = (tot / cnt - ubuf[HALO:HALO + tm, sl]).astype(BF)
        a_parts.append(jnp.dot(diff, mix_ref[g], preferred_element_type=F32))
    a_out = (jnp.concatenate(a_parts, axis=-1) * ps_ref[...]).astype(BF)

    branches = (a_out, b_ref[0], m_ref[0])
    merged = None
    for n in range(N_BRANCH):
        sl = slice(n * D_MODEL, (n + 1) * D_MODEL)
        gate = jax.nn.sigmoid(jnp.dot(h, wg_ref[:, sl], preferred_element_type=F32) + bg_ref[:, sl])
        term = gate * jnp.dot(branches[n], wbr_ref[n], preferred_element_type=F32)
        merged = term if merged is None else merged + term
    y = jnp.dot(merged.astype(BF), wo_ref[...], preferred_element_type=F32)
    o_ref[0] = x + _rms(y, lnp_ref[...])


def _merge(x, u, b_out, m_out, ln, wg, bg, mix, ps, wbr, wo, lnp):
    B, S, _ = x.shape
    tm = TM_MERGE
    hb = tm // HALO
    n_halo = S // HALO
    row = lambda b, i: (b, i, 0)
    return pl.pallas_call(
        functools.partial(_merge_kernel, seq=S),
        grid=(B, S // tm),
        in_specs=[
            pl.BlockSpec((1, tm, D_MODEL), row),
            pl.BlockSpec((1, tm, POOL_WIDTH), row),
            pl.BlockSpec((1, HALO, POOL_WIDTH), lambda b, i: (b, jnp.maximum(i * hb - 1, 0), 0)),
            pl.BlockSpec((1, HALO, POOL_WIDTH), lambda b, i: (b, jnp.minimum((i + 1) * hb, n_halo - 1), 0)),
            pl.BlockSpec((1, tm, MLA_WIDTH), row),
            pl.BlockSpec((1, tm, X_WIDTH), row),
            _const_spec((1, D_MODEL)),
            _const_spec((D_MODEL, N_BRANCH * D_MODEL)),
            _const_spec((1, N_BRANCH * D_MODEL)),
            _const_spec((POOL_GROUPS, POOL_GROUP_W, POOL_GROUP_W)),
            _const_spec((1, POOL_WIDTH)),
            _const_spec((N_BRANCH, BRANCH_WIDTH, D_MODEL)),
            _const_spec((D_MODEL, D_MODEL)),
            _const_spec((1, D_MODEL)),
        ],
        out_specs=pl.BlockSpec((1, tm, D_MODEL), row),
        out_shape=jax.ShapeDtypeStruct((B, S, D_MODEL), F32),
        scratch_shapes=[pltpu.VMEM((tm + 2 * HALO, POOL_WIDTH), F32)],
        name="merge",
        compiler_params=_params(),
    )(x, u, u, u, b_out, m_out, ln, wg, bg, mix, ps, wbr, wo, lnp)


def _ffn_kernel(x_ref, ln_ref, wgu_ref, wd_ref, lnp_ref, o_ref):
    x = x_ref[...]
    h = _rms(x, ln_ref[...]).astype(BF)
    f = None
    for c in range(D_FF // FF_CHUNK):
        g = jnp.dot(h, wgu_ref[:, c * FF_CHUNK:(c + 1) * FF_CHUNK], preferred_element_type=F32)
        u = jnp.dot(h, wgu_ref[:, D_FF + c * FF_CHUNK:D_FF + (c + 1) * FF_CHUNK], preferred_element_type=F32)
        a = (g * jax.nn.sigmoid(g) * u).astype(BF)
        part = jnp.dot(a, wd_ref[c * FF_CHUNK:(c + 1) * FF_CHUNK, :], preferred_element_type=F32)
        f = part if f is None else f + part
    o_ref[...] = x + _rms(f, lnp_ref[...])


def _ffn(x, ln, wgu, wd, lnp):
    B, S, _ = x.shape
    n = B * S
    tm = TM_FFN
    out = pl.pallas_call(
        _ffn_kernel,
        grid=(n // tm,),
        in_specs=[
            pl.BlockSpec((tm, D_MODEL), lambda i: (i, 0)),
            _const_spec((1, D_MODEL)),
            _const_spec((D_MODEL, 2 * D_FF)),
            _const_spec((D_FF, D_MODEL)),
            _const_spec((1, D_MODEL)),
        ],
        out_specs=pl.BlockSpec((tm, D_MODEL), lambda i: (i, 0)),
        out_shape=jax.ShapeDtypeStruct((n, D_MODEL), F32),
        name="ffn",
        compiler_params=_params(),
    )(x.reshape(n, D_MODEL), ln, wgu, wd, lnp)
    return out.reshape(B, S, D_MODEL)


def _rope_tables(seq):
    inv = 1.0 / (ROPE_THETA ** (np.arange(0, QK_ROPE, 2, dtype=np.float32) / QK_ROPE))
    ang = np.arange(seq, dtype=np.float32)[:, None] * inv[None, :]
    cos, sin = np.cos(ang), np.sin(ang)
    ct = np.zeros((seq, LANES), np.float32)
    sa = np.zeros((seq, LANES), np.float32)
    sb = np.zeros((seq, LANES), np.float32)
    x1 = slice(QK_NOPE, QK_NOPE + ROPE_HALF)
    x2 = slice(QK_NOPE + ROPE_HALF, QK_NOPE + QK_ROPE)
    ct[:, :QK_NOPE] = 1.0
    ct[:, x1] = cos
    ct[:, x2] = cos
    sa[:, x2] = sin
    sb[:, x1] = -sin
    return jnp.asarray(ct), jnp.asarray(sa), jnp.asarray(sb)


def _head_slabs(w, width, offset):
    L, K, _ = w.shape
    w = w.reshape(L, K, MLA_HEADS, width)
    w = jnp.pad(w, ((0, 0), (0, 0), (0, 0), (offset, HEAD_PAD - width - offset)))
    return w.reshape(L, K, MLA_HEADS * HEAD_PAD)


def _prep_weights(w_in, w_uq, w_uk, w_uv):
    pool, cq, ckv, kr, qx = (w_in[..., a:b] for a, b in
                             ((0, 512), (512, 896), (896, 1152), (1152, 1184), (1184, 1696)))
    kr_slab = jnp.pad(kr, ((0, 0), (0, 0), (QK_NOPE, LANES - QK_NOPE - QK_ROPE)))
    w_in_p = jnp.concatenate([pool, cq, ckv, qx, kr_slab], axis=-1).astype(BF)
    w_uq_p = _head_slabs(w_uq, QK_NOPE + QK_ROPE, 0).astype(BF)
    w_uk_p = _head_slabs(w_uk, QK_NOPE, 0).astype(BF)
    v_even = _head_slabs(w_uv, V_HEAD, 0).reshape(DEPTH, KV_LORA, MLA_HEADS, HEAD_PAD)
    v_odd = _head_slabs(w_uv, V_HEAD, V_HEAD).reshape(DEPTH, KV_LORA, MLA_HEADS, HEAD_PAD)
    odd = (jnp.arange(MLA_HEADS) % 2 == 1)[None, None, :, None]
    w_uv_p = jnp.where(odd, v_odd, v_even).reshape(DEPTH, KV_LORA, MLA_HEADS * HEAD_PAD).astype(BF)
    vone = np.zeros((MLA_HEADS, HEAD_PAD), np.float32)
    vone[0::2, V_HEAD] = 1.0
    vone[1::2, 0] = 1.0
    return w_in_p, w_uq_p, w_uk_p, w_uv_p, jnp.asarray(vone.reshape(1, -1))


def kernel(x_prompt, x_sample, mem_prompt, mem_sample, w_in, q_norm, kv_norm, w_uq, w_uk, w_uv, pool_mix,
           pool_scale, mem_norm, w_mem_kv, w_branch, w_gate, b_gate, w_out, ln_mix_pre, ln_mix_post,
           ln_ffn_pre, ln_ffn_post, w_gu, w_down):
    w_in_p, w_uq_p, w_uk_p, w_uv_p, vone = _prep_weights(w_in, w_uq, w_uk, w_uv)
    w_mem_kv_b = w_mem_kv.astype(BF)
    w_gate_b, w_branch_b, w_out_b = w_gate.astype(BF), w_branch.astype(BF), w_out.astype(BF)
    pool_mix_b, w_gu_b, w_down_b = pool_mix.astype(BF), w_gu.astype(BF), w_down.astype(BF)
    vec = lambda a, l: a[l][None, :]

    def trunk(x, mem):
        tabs = _rope_tables(x.shape[1])
        kmem, vmem = _memkv(mem, mem_norm, w_mem_kv_b)
        for l in range(DEPTH):
            u, q, k, v, m_out = _pre(x, tabs, kmem, vmem, l, vec(ln_mix_pre, l), w_in_p[l], vec(q_norm, l),
                                     vec(kv_norm, l), w_uq_p[l], w_uk_p[l], w_uv_p[l], vone)
            b_out = _attn(q, k, v)
            x = _merge(x, u, b_out, m_out, vec(ln_mix_pre, l), w_gate_b[l], vec(b_gate, l), pool_mix_b[l],
                       vec(pool_scale, l), w_branch_b[l], w_out_b[l], vec(ln_mix_post, l))
            x = _ffn(x, vec(ln_ffn_pre, l), w_gu_b[l], w_down_b[l], vec(ln_ffn_post, l))
        return x

    return (trunk(x_prompt, mem_prompt), trunk(x_sample, mem_sample))
```

```python
import functools

import jax
import jax.numpy as jnp
import numpy as np
from jax import lax
from jax.experimental import pallas as pl
from jax.experimental.pallas import tpu as pltpu

D_MODEL = 1024
DEPTH = 4
N_MEM = 256
POOL_WINDOWS = (2, 4, 8, 16)
POOL_GROUPS = len(POOL_WINDOWS)
POOL_WIDTH = 512
POOL_GROUP_W = POOL_WIDTH // POOL_GROUPS
MLA_HEADS = 8
QK_NOPE = 64
QK_ROPE = 32
ROPE_HALF = QK_ROPE // 2
V_HEAD = 64
Q_LORA = 384
KV_LORA = 256
MLA_WIDTH = MLA_HEADS * V_HEAD
ROPE_THETA = 10000.0
X_HEADS = 4
X_HEAD_DIM = 128
X_WIDTH = X_HEADS * X_HEAD_DIM
N_BRANCH = 3
BRANCH_WIDTH = 512
D_FF = 2816
EPS = 1e-6

LANES = 128
HEAD_PAD = LANES
HALO = 16
Z_POOL = 0
Z_CQ = Z_POOL + POOL_WIDTH
Z_CKV = Z_CQ + Q_LORA
Z_QX = Z_CKV + KV_LORA
Z_KR = Z_QX + X_WIDTH
Z_WIDTH = Z_KR + LANES
FF_CHUNK = 256
LOG2E = 1.4426950408889634
VMEM_LIMIT = 56 * 1024 * 1024

TM_PRE = 1024
TM_MERGE = 1024
TM_FFN = 512
ATT_CHUNKS = 4
ATT_MIN_TK = 1024
ATT_TILE_ELEMS = 512 * 2048
ATT_MAX_TQ = 1024
SOFTMAX_ROWS = 32

BF = jnp.bfloat16
F32 = jnp.float32


def _rms(x, g):
    y = x * lax.rsqrt(jnp.mean(x * x, axis=-1, keepdims=True) + EPS)
    return y * g


def _const_spec(shape):
    nd = len(shape)
    return pl.BlockSpec(shape, lambda *_: (0,) * nd, pipeline_mode=pl.Buffered(1))


def _params():
    return pltpu.CompilerParams(vmem_limit_bytes=VMEM_LIMIT)


def _memkv_kernel(mem_ref, g_ref, w_ref, k_ref, v_ref):
    mn = _rms(mem_ref[0], g_ref[0]).astype(BF)
    kv = jnp.dot(mn, w_ref[0], preferred_element_type=F32)
    k_ref[0, 0] = kv[:, :X_WIDTH].astype(BF)
    v_ref[0, 0] = kv[:, X_WIDTH:].astype(BF)


def _memkv(mem, mem_norm, w_mem_kv):
    B = mem.shape[0]
    out = jax.ShapeDtypeStruct((DEPTH, B, N_MEM, X_WIDTH), BF)
    return pl.pallas_call(
        _memkv_kernel,
        grid=(DEPTH, B),
        in_specs=[
            pl.BlockSpec((1, N_MEM, D_MODEL), lambda l, b: (b, 0, 0)),
            pl.BlockSpec((1, 1, D_MODEL), lambda l, b: (l, 0, 0)),
            pl.BlockSpec((1, D_MODEL, 2 * X_WIDTH), lambda l, b: (l, 0, 0)),
        ],
        out_specs=[
            pl.BlockSpec((1, 1, N_MEM, X_WIDTH), lambda l, b: (l, b, 0, 0)),
            pl.BlockSpec((1, 1, N_MEM, X_WIDTH), lambda l, b: (l, b, 0, 0)),
        ],
        out_shape=[out, out],
        name="memkv",
        compiler_params=_params(),
    )(mem, mem_norm.reshape(DEPTH, 1, D_MODEL), w_mem_kv)


def _rope_slab(x, ct, sa, sb):
    return x * ct + pltpu.roll(x, ROPE_HALF, 1) * sa + pltpu.roll(x, LANES - ROPE_HALF, 1) * sb


def _pre_kernel(x_ref, ct_ref, sa_ref, sb_ref, kmem_ref, vmem_ref, ln_ref, w_in_ref, qn_ref, kvn_ref,
                w_uq_ref, w_uk_ref, w_uv_ref, vone_ref,
                u_ref, q_ref, k_ref, v_ref, m_ref):
    h = _rms(x_ref[0], ln_ref[...]).astype(BF)
    z = jnp.dot(h, w_in_ref[...], preferred_element_type=F32)
    u_ref[0] = z[:, Z_POOL:Z_POOL + POOL_WIDTH]

    ct, sa, sb = ct_ref[...], sa_ref[...], sb_ref[...]
    q_scale = (QK_NOPE + QK_ROPE) ** -0.5 * LOG2E
    cq = _rms(z[:, Z_CQ:Z_CQ + Q_LORA], qn_ref[...]).astype(BF)
    qf = jnp.dot(cq, w_uq_ref[...], preferred_element_type=F32)
    c = _rms(z[:, Z_CKV:Z_CKV + KV_LORA], kvn_ref[...]).astype(BF)
    kf = jnp.dot(c, w_uk_ref[...], preferred_element_type=F32)
    vf = jnp.dot(c, w_uv_ref[...], preferred_element_type=F32) + vone_ref[...]
    kr = _rope_slab(z[:, Z_KR:Z_KR + LANES], ct, sa, sb)
    for hd in range(MLA_HEADS):
        sl = slice(hd * HEAD_PAD, (hd + 1) * HEAD_PAD)
        q_ref[0, hd] = (_rope_slab(qf[:, sl], ct, sa, sb) * q_scale).astype(BF)
        k_ref[0, hd] = (kf[:, sl] + kr).astype(BF)
        v_ref[0, hd] = vf[:, sl].astype(BF)

    x_scale = X_HEAD_DIM ** -0.5 * LOG2E
    qx = (z[:, Z_QX:Z_QX + X_WIDTH] * x_scale).astype(BF)
    outs = []
    for hd in range(X_HEADS):
        sl = slice(hd * X_HEAD_DIM, (hd + 1) * X_HEAD_DIM)
        s = lax.dot_general(qx[:, sl], kmem_ref[0, 0, :, sl], (((1,), (1,)), ((), ())),
                            preferred_element_type=F32)
        p = jnp.exp2(s - jnp.max(s, axis=-1, keepdims=True))
        l = jnp.sum(p, axis=-1, keepdims=True)
        o = jnp.dot(p.astype(BF), vmem_ref[0, 0, :, sl], preferred_element_type=F32)
        outs.append(o / l)
    m_ref[0] = jnp.concatenate(outs, axis=-1).astype(BF)


def _pre(x, tabs, kmem, vmem, l, ln, w_in, qn, kvn, w_uq, w_uk, w_uv, vone):
    B, S, _ = x.shape
    tm = TM_PRE
    row = lambda b, i: (b, i, 0)
    head = lambda b, i: (b, 0, i, 0)
    tab = lambda b, i: (i, 0)
    att = jax.ShapeDtypeStruct((B, MLA_HEADS, S, HEAD_PAD), BF)
    return pl.pallas_call(
        _pre_kernel,
        grid=(B, S // tm),
        in_specs=[
            pl.BlockSpec((1, tm, D_MODEL), row),
            pl.BlockSpec((tm, LANES), tab), pl.BlockSpec((tm, LANES), tab), pl.BlockSpec((tm, LANES), tab),
            pl.BlockSpec((1, 1, N_MEM, X_WIDTH), lambda b, i: (l, b, 0, 0)),
            pl.BlockSpec((1, 1, N_MEM, X_WIDTH), lambda b, i: (l, b, 0, 0)),
            _const_spec((1, D_MODEL)), _const_spec((D_MODEL, Z_WIDTH)),
            _const_spec((1, Q_LORA)), _const_spec((1, KV_LORA)),
            _const_spec((Q_LORA, MLA_HEADS * HEAD_PAD)),
            _const_spec((KV_LORA, MLA_HEADS * HEAD_PAD)),
            _const_spec((KV_LORA, MLA_HEADS * HEAD_PAD)),
            _const_spec((1, MLA_HEADS * HEAD_PAD)),
        ],
        out_specs=[
            pl.BlockSpec((1, tm, POOL_WIDTH), row),
            pl.BlockSpec((1, MLA_HEADS, tm, HEAD_PAD), head),
            pl.BlockSpec((1, MLA_HEADS, tm, HEAD_PAD), head),
            pl.BlockSpec((1, MLA_HEADS, tm, HEAD_PAD), head),
            pl.BlockSpec((1, tm, X_WIDTH), row),
        ],
        out_shape=[
            jax.ShapeDtypeStruct((B, S, POOL_WIDTH), F32), att, att, att,
            jax.ShapeDtypeStruct((B, S, X_WIDTH), BF),
        ],
        name="pre",
        compiler_params=_params(),
    )(x, *tabs, kmem, vmem, ln, w_in, qn, kvn, w_uq, w_uk, w_uv, vone)


def _attn_kernel(q_ref, k_ref, v_ref, o_ref, s_scr, p_scr, m_scr, a_scr, acc_scr, *, n_chunks, tk):
    tq = q_ref.shape[2]
    heads = range(q_ref.shape[1])
    nt = (((1,), (1,)), ((), ()))

    def scores(j, c):
        s_scr[j, c % 2] = lax.dot_general(q_ref[0, j], k_ref[0, j, c * tk:(c + 1) * tk, :], nt,
                                          preferred_element_type=F32)

    def softmax(j, c):
        slot = c % 2
        for r in range(tq // SOFTMAX_ROWS):
            rows = slice(r * SOFTMAX_ROWS, (r + 1) * SOFTMAX_ROWS)
            mx = s_scr[j, slot, rows, 0:LANES]
            for kb in range(1, tk // LANES):
                mx = jnp.maximum(mx, s_scr[j, slot, rows, kb * LANES:(kb + 1) * LANES])
            mx = jnp.broadcast_to(jnp.max(mx, axis=-1, keepdims=True), (SOFTMAX_ROWS, LANES))
            if c == 0:
                m_new = mx
            else:
                m_old = m_scr[j, rows, :]
                m_new = jnp.maximum(m_old, mx)
                a_scr[j, slot, rows, :] = jnp.exp2(m_old - m_new)
            for kb in range(tk // LANES):
                cols = slice(kb * LANES, (kb + 1) * LANES)
                p_scr[j, slot, rows, cols] = jnp.exp2(s_scr[j, slot, rows, cols] - m_new).astype(BF)
            m_scr[j, rows, :] = m_new

    def values(j, c):
        slot = c % 2
        pv = jnp.dot(p_scr[j, slot], v_ref[0, j, c * tk:(c + 1) * tk, :], preferred_element_type=F32)
        acc_scr[j] = pv if c == 0 else acc_scr[j] * a_scr[j, slot] + pv

    for j in heads:
        scores(j, 0)
    for c in range(n_chunks):
        for j in heads:
            if c + 1 < n_chunks:
                scores(j, c + 1)
            softmax(j, c)
            if c >= 1:
                values(j, c - 1)
    for j in heads:
        values(j, n_chunks - 1)
    o0 = acc_scr[0] / acc_scr[0][:, V_HEAD:V_HEAD + 1]
    o1 = acc_scr[1] / acc_scr[1][:, 0:1]
    lane = lax.broadcasted_iota(jnp.int32, (tq, HEAD_PAD), 1)
    o_ref[0] = jnp.where(lane < V_HEAD, o0, o1).astype(BF)


def _attn(q, k, v):
    B, H, S, _ = q.shape
    tk = max(S // ATT_CHUNKS, ATT_MIN_TK)
    tq = min(ATT_TILE_ELEMS // tk, ATT_MAX_TQ)
    assert S % tk == 0 and tk % LANES == 0 and S % tq == 0 and tq % SOFTMAX_ROWS == 0
    kern = functools.partial(_attn_kernel, n_chunks=S // tk, tk=tk)
    return pl.pallas_call(
        kern,
        grid=(B, H // 2, S // tq),
        in_specs=[
            pl.BlockSpec((1, 2, tq, HEAD_PAD), lambda b, hp, i: (b, hp, i, 0)),
            pl.BlockSpec((1, 2, S, HEAD_PAD), lambda b, hp, i: (b, hp, 0, 0)),
            pl.BlockSpec((1, 2, S, HEAD_PAD), lambda b, hp, i: (b, hp, 0, 0)),
        ],
        out_specs=pl.BlockSpec((1, tq, 2 * V_HEAD), lambda b, hp, i: (b, i, hp)),
        out_shape=jax.ShapeDtypeStruct((B, S, MLA_WIDTH), BF),
        scratch_shapes=[
            pltpu.VMEM((2, 2, tq, tk), F32),
            pltpu.VMEM((2, 2, tq, tk), BF),
            pltpu.VMEM((2, tq, LANES), F32),
            pltpu.VMEM((2, 2, tq, LANES), F32),
            pltpu.VMEM((2, tq, HEAD_PAD), F32),
        ],
        name="attn",
        compiler_params=_params(),
    )(q, k, v)


def _window_sum(ubuf, pbuf, lanes, w, tm):
    half = w // 2
    src, rows, m, slot = (lambda a, b: ubuf[a:b, lanes]), tm + 3 * HALO, 1, 0
    while m < half:
        rows -= 8
        pbuf[slot, 0:rows, :] = src(0, rows) + src(m, rows + m)
        src = functools.partial(lambda s, a, b: pbuf[s, a:b, :], slot)
        m, slot = 2 * m, 1 - slot
    return src(HALO - half, HALO - half + tm) + src(HALO, HALO + tm)


def _merge_kernel(x_ref, u_ref, up_ref, un_ref, icnt_ref, b_ref, m_ref, ln_ref, wg_ref, bg_ref, mix_ref, ps_ref,
                  wbr_ref, wo_ref, lnp_ref, o_ref, ubuf, pbuf):
    tm = x_ref.shape[1]
    i = pl.program_id(1)
    x = x_ref[0]
    h = _rms(x, ln_ref[...]).astype(BF)

    ubuf[0:HALO, :] = jnp.where(i > 0, up_ref[0], 0.0)
    ubuf[HALO:HALO + tm, :] = u_ref[0]
    ubuf[HALO + tm:2 * HALO + tm, :] = jnp.where(i < pl.num_programs(1) - 1, un_ref[0], 0.0)
    ubuf[2 * HALO + tm:, :] = jnp.zeros((HALO, POOL_WIDTH), F32)
    icnt = icnt_ref[...]
    a_parts = []
    for g, w in enumerate(POOL_WINDOWS):
        sl = slice(g * POOL_GROUP_W, (g + 1) * POOL_GROUP_W)
        tot = _window_sum(ubuf, pbuf, sl, w, tm)
        diff = (tot * icnt[:, g:g + 1] - ubuf[HALO:HALO + tm, sl]).astype(BF)
        a_parts.append(jnp.dot(diff, mix_ref[g], preferred_element_type=F32))
    a_out = (jnp.concatenate(a_parts, axis=-1) * ps_ref[...]).astype(BF)

    branches = (a_out, b_ref[0], m_ref[0])
    merged = None
    for n in range(N_BRANCH):
        sl = slice(n * D_MODEL, (n + 1) * D_MODEL)
        gate = jax.nn.sigmoid(jnp.dot(h, wg_ref[:, sl], preferred_element_type=F32) + bg_ref[:, sl])
        term = gate * jnp.dot(branches[n], wbr_ref[n], preferred_element_type=F32)
        merged = term if merged is None else merged + term
    y = jnp.dot(merged.astype(BF), wo_ref[...], preferred_element_type=F32)
    o_ref[0] = x + _rms(y, lnp_ref[...])


def _merge(x, u, b_out, m_out, ln, wg, bg, mix, ps, wbr, wo, lnp):
    B, S, _ = x.shape
    tm = TM_MERGE
    hb = tm // HALO
    n_halo = S // HALO
    row = lambda b, i: (b, i, 0)
    t = np.arange(S)
    icnt = np.stack([1.0 / (np.minimum(t + w // 2, S) - np.maximum(t - w // 2, 0)) for w in POOL_WINDOWS], axis=1)
    return pl.pallas_call(
        _merge_kernel,
        grid=(B, S // tm),
        in_specs=[
            pl.BlockSpec((1, tm, D_MODEL), row),
            pl.BlockSpec((1, tm, POOL_WIDTH), row),
            pl.BlockSpec((1, HALO, POOL_WIDTH), lambda b, i: (b, jnp.maximum(i * hb - 1, 0), 0)),
            pl.BlockSpec((1, HALO, POOL_WIDTH), lambda b, i: (b, jnp.minimum((i + 1) * hb, n_halo - 1), 0)),
            pl.BlockSpec((tm, POOL_GROUPS), lambda b, i: (i, 0)),
            pl.BlockSpec((1, tm, MLA_WIDTH), row),
            pl.BlockSpec((1, tm, X_WIDTH), row),
            _const_spec((1, D_MODEL)),
            _const_spec((D_MODEL, N_BRANCH * D_MODEL)),
            _const_spec((1, N_BRANCH * D_MODEL)),
            _const_spec((POOL_GROUPS, POOL_GROUP_W, POOL_GROUP_W)),
            _const_spec((1, POOL_WIDTH)),
            _const_spec((N_BRANCH, BRANCH_WIDTH, D_MODEL)),
            _const_spec((D_MODEL, D_MODEL)),
            _const_spec((1, D_MODEL)),
        ],
        out_specs=pl.BlockSpec((1, tm, D_MODEL), row),
        out_shape=jax.ShapeDtypeStruct((B, S, D_MODEL), F32),
        scratch_shapes=[
            pltpu.VMEM((tm + 3 * HALO, POOL_WIDTH), F32),
            pltpu.VMEM((2, tm + 3 * HALO - 8, POOL_GROUP_W), F32),
        ],
        name="merge",
        compiler_params=_params(),
    )(x, u, u, u, jnp.asarray(icnt, F32), b_out, m_out, ln, wg, bg, mix, ps, wbr, wo, lnp)


def _ffn_kernel(x_ref, ln_ref, wgu_ref, wd_ref, lnp_ref, o_ref):
    x = x_ref[...]
    h = _rms(x, ln_ref[...]).astype(BF)
    f = None
    for c in range(D_FF // FF_CHUNK):
        g = jnp.dot(h, wgu_ref[:, c * FF_CHUNK:(c + 1) * FF_CHUNK], preferred_element_type=F32)
        u = jnp.dot(h, wgu_ref[:, D_FF + c * FF_CHUNK:D_FF + (c + 1) * FF_CHUNK], preferred_element_type=F32)
        a = (g * jax.nn.sigmoid(g) * u).astype(BF)
        part = jnp.dot(a, wd_ref[c * FF_CHUNK:(c + 1) * FF_CHUNK, :], preferred_element_type=F32)
        f = part if f is None else f + part
    o_ref[...] = x + _rms(f, lnp_ref[...])


def _ffn(x, ln, wgu, wd, lnp):
    B, S, _ = x.shape
    n = B * S
    tm = TM_FFN
    out = pl.pallas_call(
        _ffn_kernel,
        grid=(n // tm,),
        in_specs=[
            pl.BlockSpec((tm, D_MODEL), lambda i: (i, 0)),
            _const_spec((1, D_MODEL)),
            _const_spec((D_MODEL, 2 * D_FF)),
            _const_spec((D_FF, D_MODEL)),
            _const_spec((1, D_MODEL)),
        ],
        out_specs=pl.BlockSpec((tm, D_MODEL), lambda i: (i, 0)),
        out_shape=jax.ShapeDtypeStruct((n, D_MODEL), F32),
        name="ffn",
        compiler_params=_params(),
    )(x.reshape(n, D_MODEL), ln, wgu, wd, lnp)
    return out.reshape(B, S, D_MODEL)


def _rope_tables(seq):
    inv = 1.0 / (ROPE_THETA ** (np.arange(0, QK_ROPE, 2, dtype=np.float32) / QK_ROPE))
    ang = np.arange(seq, dtype=np.float32)[:, None] * inv[None, :]
    cos, sin = np.cos(ang), np.sin(ang)
    ct = np.zeros((seq, LANES), np.float32)
    sa = np.zeros((seq, LANES), np.float32)
    sb = np.zeros((seq, LANES), np.float32)
    x1 = slice(QK_NOPE, QK_NOPE + ROPE_HALF)
    x2 = slice(QK_NOPE + ROPE_HALF, QK_NOPE + QK_ROPE)
    ct[:, :QK_NOPE] = 1.0
    ct[:, x1] = cos
    ct[:, x2] = cos
    sa[:, x2] = sin
    sb[:, x1] = -sin
    return jnp.asarray(ct), jnp.asarray(sa), jnp.asarray(sb)


def _head_slabs(w, width, offset):
    L, K, _ = w.shape
    w = w.reshape(L, K, MLA_HEADS, width)
    w = jnp.pad(w, ((0, 0), (0, 0), (0, 0), (offset, HEAD_PAD - width - offset)))
    return w.reshape(L, K, MLA_HEADS * HEAD_PAD)


def _prep_weights(w_in, w_uq, w_uk, w_uv):
    pool, cq, ckv, kr, qx = (w_in[..., a:b] for a, b in
                             ((0, 512), (512, 896), (896, 1152), (1152, 1184), (1184, 1696)))
    kr_slab = jnp.pad(kr, ((0, 0), (0, 0), (QK_NOPE, LANES - QK_NOPE - QK_ROPE)))
    w_in_p = jnp.concatenate([pool, cq, ckv, qx, kr_slab], axis=-1).astype(BF)
    w_uq_p = _head_slabs(w_uq, QK_NOPE + QK_ROPE, 0).astype(BF)
    w_uk_p = _head_slabs(w_uk, QK_NOPE, 0).astype(BF)
    v_even = _head_slabs(w_uv, V_HEAD, 0).reshape(DEPTH, KV_LORA, MLA_HEADS, HEAD_PAD)
    v_odd = _head_slabs(w_uv, V_HEAD, V_HEAD).reshape(DEPTH, KV_LORA, MLA_HEADS, HEAD_PAD)
    odd = (jnp.arange(MLA_HEADS) % 2 == 1)[None, None, :, None]
    w_uv_p = jnp.where(odd, v_odd, v_even).reshape(DEPTH, KV_LORA, MLA_HEADS * HEAD_PAD).astype(BF)
    vone = np.zeros((MLA_HEADS, HEAD_PAD), np.float32)
    vone[0::2, V_HEAD] = 1.0
    vone[1::2, 0] = 1.0
    return w_in_p, w_uq_p, w_uk_p, w_uv_p, jnp.asarray(vone.reshape(1, -1))


def kernel(x_prompt, x_sample, mem_prompt, mem_sample, w_in, q_norm, kv_norm, w_uq, w_uk, w_uv, pool_mix,
           pool_scale, mem_norm, w_mem_kv, w_branch, w_gate, b_gate, w_out, ln_mix_pre, ln_mix_post,
           ln_ffn_pre, ln_ffn_post, w_gu, w_down):
    w_in_p, w_uq_p, w_uk_p, w_uv_p, vone = _prep_weights(w_in, w_uq, w_uk, w_uv)
    w_mem_kv_b = w_mem_kv.astype(BF)
    w_gate_b, w_branch_b, w_out_b = w_gate.astype(BF), w_branch.astype(BF), w_out.astype(BF)
    pool_mix_b, w_gu_b, w_down_b = pool_mix.astype(BF), w_gu.astype(BF), w_down.astype(BF)
    vec = lambda a, l: a[l][None, :]

    def trunk(x, mem):
        tabs = _rope_tables(x.shape[1])
        kmem, vmem = _memkv(mem, mem_norm, w_mem_kv_b)
        for l in range(DEPTH):
            u, q, k, v, m_out = _pre(x, tabs, kmem, vmem, l, vec(ln_mix_pre, l), w_in_p[l], vec(q_norm, l),
                                     vec(kv_norm, l), w_uq_p[l], w_uk_p[l], w_uv_p[l], vone)
            b_out = _attn(q, k, v)
            x = _merge(x, u, b_out, m_out, vec(ln_mix_pre, l), w_gate_b[l], vec(b_gate, l), pool_mix_b[l],
                       vec(pool_scale, l), w_branch_b[l], w_out_b[l], vec(ln_mix_post, l))
            x = _ffn(x, vec(ln_ffn_pre, l), w_gu_b[l], w_down_b[l], vec(ln_ffn_post, l))
        return x

    return (trunk(x_prompt, mem_prompt), trunk(x_sample, mem_sample))
```

```python
import functools

import jax
import jax.numpy as jnp
import numpy as np
from jax import lax
from jax.experimental import pallas as pl
from jax.experimental.pallas import tpu as pltpu

D_MODEL = 1024
DEPTH = 4
N_MEM = 256
POOL_WINDOWS = (2, 4, 8, 16)
POOL_GROUPS = len(POOL_WINDOWS)
POOL_WIDTH = 512
POOL_GROUP_W = POOL_WIDTH // POOL_GROUPS
MLA_HEADS = 8
QK_NOPE = 64
QK_ROPE = 32
ROPE_HALF = QK_ROPE // 2
V_HEAD = 64
Q_LORA = 384
KV_LORA = 256
MLA_WIDTH = MLA_HEADS * V_HEAD
ROPE_THETA = 10000.0
X_HEADS = 4
X_HEAD_DIM = 128
X_WIDTH = X_HEADS * X_HEAD_DIM
N_BRANCH = 3
BRANCH_WIDTH = 512
D_FF = 2816
EPS = 1e-6

LANES = 128
HEAD_PAD = LANES
HALO = 16
Z_POOL = 0
Z_CQ = Z_POOL + POOL_WIDTH
Z_CKV = Z_CQ + Q_LORA
Z_QX = Z_CKV + KV_LORA
Z_KR = Z_QX + X_WIDTH
Z_WIDTH = Z_KR + LANES
FF_CHUNK = 256
LOG2E = 1.4426950408889634
VMEM_LIMIT = 56 * 1024 * 1024

TM_PRE = 1024
TM_MERGE = 1024
TM_FFN = 1024
ATT_CHUNKS = 4
ATT_MIN_TK = 1024
ATT_TILE_ELEMS = 512 * 2048
ATT_MAX_TQ = 1024
SOFTMAX_ROWS = 32

BF = jnp.bfloat16
F32 = jnp.float32


def _rms(x, g):
    y = x * lax.rsqrt(jnp.mean(x * x, axis=-1, keepdims=True) + EPS)
    return y * g


def _const_spec(shape):
    nd = len(shape)
    return pl.BlockSpec(shape, lambda *_: (0,) * nd, pipeline_mode=pl.Buffered(1))


def _params():
    return pltpu.CompilerParams(vmem_limit_bytes=VMEM_LIMIT)


def _memkv_kernel(mem_ref, g_ref, w_ref, k_ref, v_ref):
    mn = _rms(mem_ref[0], g_ref[0]).astype(BF)
    kv = jnp.dot(mn, w_ref[0], preferred_element_type=F32)
    k_ref[0, 0] = kv[:, :X_WIDTH].astype(BF)
    v_ref[0, 0] = kv[:, X_WIDTH:].astype(BF)


def _memkv(mem, mem_norm, w_mem_kv):
    B = mem.shape[0]
    out = jax.ShapeDtypeStruct((DEPTH, B, N_MEM, X_WIDTH), BF)
    return pl.pallas_call(
        _memkv_kernel,
        grid=(DEPTH, B),
        in_specs=[
            pl.BlockSpec((1, N_MEM, D_MODEL), lambda l, b: (b, 0, 0)),
            pl.BlockSpec((1, 1, D_MODEL), lambda l, b: (l, 0, 0)),
            pl.BlockSpec((1, D_MODEL, 2 * X_WIDTH), lambda l, b: (l, 0, 0)),
        ],
        out_specs=[
            pl.BlockSpec((1, 1, N_MEM, X_WIDTH), lambda l, b: (l, b, 0, 0)),
            pl.BlockSpec((1, 1, N_MEM, X_WIDTH), lambda l, b: (l, b, 0, 0)),
        ],
        out_shape=[out, out],
        name="memkv",
        compiler_params=_params(),
    )(mem, mem_norm.reshape(DEPTH, 1, D_MODEL), w_mem_kv)


def _rope_slab(x, ct, sa, sb):
    return x * ct + pltpu.roll(x, ROPE_HALF, 1) * sa + pltpu.roll(x, LANES - ROPE_HALF, 1) * sb


def _pre_kernel(x_ref, ct_ref, sa_ref, sb_ref, kmem_ref, vmem_ref, ln_ref, w_in_ref, qn_ref, kvn_ref,
                w_uq_ref, w_uk_ref, w_uv_ref, vone_ref,
                u_ref, q_ref, k_ref, v_ref, m_ref):
    h = _rms(x_ref[0], ln_ref[...]).astype(BF)
    z = jnp.dot(h, w_in_ref[...], preferred_element_type=F32)
    u_ref[0] = z[:, Z_POOL:Z_POOL + POOL_WIDTH]

    ct, sa, sb = ct_ref[...], sa_ref[...], sb_ref[...]
    q_scale = (QK_NOPE + QK_ROPE) ** -0.5 * LOG2E
    cq = _rms(z[:, Z_CQ:Z_CQ + Q_LORA], qn_ref[...]).astype(BF)
    qf = jnp.dot(cq, w_uq_ref[...], preferred_element_type=F32)
    c = _rms(z[:, Z_CKV:Z_CKV + KV_LORA], kvn_ref[...]).astype(BF)
    kf = jnp.dot(c, w_uk_ref[...], preferred_element_type=F32)
    vf = jnp.dot(c, w_uv_ref[...], preferred_element_type=F32) + vone_ref[...]
    kr = _rope_slab(z[:, Z_KR:Z_KR + LANES], ct, sa, sb)
    for hd in range(MLA_HEADS):
        sl = slice(hd * HEAD_PAD, (hd + 1) * HEAD_PAD)
        q_ref[0, hd] = (_rope_slab(qf[:, sl], ct, sa, sb) * q_scale).astype(BF)
        k_ref[0, hd] = (kf[:, sl] + kr).astype(BF)
        v_ref[0, hd] = vf[:, sl].astype(BF)

    x_scale = X_HEAD_DIM ** -0.5 * LOG2E
    qx = (z[:, Z_QX:Z_QX + X_WIDTH] * x_scale).astype(BF)
    outs = []
    for hd in range(X_HEADS):
        sl = slice(hd * X_HEAD_DIM, (hd + 1) * X_HEAD_DIM)
        s = lax.dot_general(qx[:, sl], kmem_ref[0, 0, :, sl], (((1,), (1,)), ((), ())),
                            preferred_element_type=F32)
        p = jnp.exp2(s - jnp.max(s, axis=-1, keepdims=True))
        l = jnp.sum(p, axis=-1, keepdims=True)
        o = jnp.dot(p.astype(BF), vmem_ref[0, 0, :, sl], preferred_element_type=F32)
        outs.append(o / l)
    m_ref[0] = jnp.concatenate(outs, axis=-1).astype(BF)


def _pre(x, tabs, kmem, vmem, l, ln, w_in, qn, kvn, w_uq, w_uk, w_uv, vone):
    B, S, _ = x.shape
    tm = TM_PRE
    row = lambda b, i: (b, i, 0)
    head = lambda b, i: (b, 0, i, 0)
    tab = lambda b, i: (i, 0)
    att = jax.ShapeDtypeStruct((B, MLA_HEADS, S, HEAD_PAD), BF)
    return pl.pallas_call(
        _pre_kernel,
        grid=(B, S // tm),
        in_specs=[
            pl.BlockSpec((1, tm, D_MODEL), row),
            pl.BlockSpec((tm, LANES), tab), pl.BlockSpec((tm, LANES), tab), pl.BlockSpec((tm, LANES), tab),
            pl.BlockSpec((1, 1, N_MEM, X_WIDTH), lambda b, i: (l, b, 0, 0)),
            pl.BlockSpec((1, 1, N_MEM, X_WIDTH), lambda b, i: (l, b, 0, 0)),
            _const_spec((1, D_MODEL)), _const_spec((D_MODEL, Z_WIDTH)),
            _const_spec((1, Q_LORA)), _const_spec((1, KV_LORA)),
            _const_spec((Q_LORA, MLA_HEADS * HEAD_PAD)),
            _const_spec((KV_LORA, MLA_HEADS * HEAD_PAD)),
            _const_spec((KV_LORA, MLA_HEADS * HEAD_PAD)),
            _const_spec((1, MLA_HEADS * HEAD_PAD)),
        ],
        out_specs=[
            pl.BlockSpec((1, tm, POOL_WIDTH), row),
            pl.BlockSpec((1, MLA_HEADS, tm, HEAD_PAD), head),
            pl.BlockSpec((1, MLA_HEADS, tm, HEAD_PAD), head),
            pl.BlockSpec((1, MLA_HEADS, tm, HEAD_PAD), head),
            pl.BlockSpec((1, tm, X_WIDTH), row),
        ],
        out_shape=[
            jax.ShapeDtypeStruct((B, S, POOL_WIDTH), F32), att, att, att,
            jax.ShapeDtypeStruct((B, S, X_WIDTH), BF),
        ],
        name="pre",
        compiler_params=_params(),
    )(x, *tabs, kmem, vmem, ln, w_in, qn, kvn, w_uq, w_uk, w_uv, vone)


def _attn_kernel(q_ref, k_ref, v_ref, o_ref, s_scr, p_scr, m_scr, a_scr, acc_scr, *, n_chunks, tk, tq):
    heads = range(q_ref.shape[1])
    nt = (((1,), (1,)), ((), ()))
    lax.fori_loop(0, q_ref.shape[2] // tq,
                  functools.partial(_attn_tile, q_ref, k_ref, v_ref, o_ref, s_scr, p_scr, m_scr, a_scr, acc_scr,
                                    heads, nt, n_chunks, tk, tq), 0)


def _attn_tile(q_ref, k_ref, v_ref, o_ref, s_scr, p_scr, m_scr, a_scr, acc_scr, heads, nt, n_chunks, tk, tq, t, carry):
    row0 = pl.multiple_of(t * tq, tq)

    def scores(j, c):
        s_scr[j, c % 2] = lax.dot_general(q_ref[0, j, pl.ds(row0, tq), :], k_ref[0, j, c * tk:(c + 1) * tk, :], nt,
                                          preferred_element_type=F32)

    def softmax(j, c):
        slot = c % 2
        for r in range(tq // SOFTMAX_ROWS):
            rows = slice(r * SOFTMAX_ROWS, (r + 1) * SOFTMAX_ROWS)
            mx = s_scr[j, slot, rows, 0:LANES]
            for kb in range(1, tk // LANES):
                mx = jnp.maximum(mx, s_scr[j, slot, rows, kb * LANES:(kb + 1) * LANES])
            mx = jnp.broadcast_to(jnp.max(mx, axis=-1, keepdims=True), (SOFTMAX_ROWS, LANES))
            if c == 0:
                m_new = mx
            else:
                m_old = m_scr[j, rows, :]
                m_new = jnp.maximum(m_old, mx)
                a_scr[j, slot, rows, :] = jnp.exp2(m_old - m_new)
            for kb in range(tk // LANES):
                cols = slice(kb * LANES, (kb + 1) * LANES)
                p_scr[j, slot, rows, cols] = jnp.exp2(s_scr[j, slot, rows, cols] - m_new).astype(BF)
            m_scr[j, rows, :] = m_new

    def values(j, c):
        slot = c % 2
        pv = jnp.dot(p_scr[j, slot], v_ref[0, j, c * tk:(c + 1) * tk, :], preferred_element_type=F32)
        acc_scr[j] = pv if c == 0 else acc_scr[j] * a_scr[j, slot] + pv

    for j in heads:
        scores(j, 0)
    for c in range(n_chunks):
        for j in heads:
            if c + 1 < n_chunks:
                scores(j, c + 1)
            softmax(j, c)
            if c >= 1:
                values(j, c - 1)
    for j in heads:
        values(j, n_chunks - 1)
    o0 = acc_scr[0] / acc_scr[0][:, V_HEAD:V_HEAD + 1]
    o1 = acc_scr[1] / acc_scr[1][:, 0:1]
    lane = lax.broadcasted_iota(jnp.int32, (tq, HEAD_PAD), 1)
    o_ref[0, pl.ds(row0, tq), :] = jnp.where(lane < V_HEAD, o0, o1).astype(BF)
    return carry


def _attn(q, k, v):
    B, H, S, _ = q.shape
    tk = max(S // ATT_CHUNKS, ATT_MIN_TK)
    tq = min(ATT_TILE_ELEMS // tk, ATT_MAX_TQ)
    assert S % tk == 0 and tk % LANES == 0 and S % tq == 0 and tq % SOFTMAX_ROWS == 0
    kern = functools.partial(_attn_kernel, n_chunks=S // tk, tk=tk, tq=tq)
    whole = lambda b, hp: (b, hp, 0, 0)
    return pl.pallas_call(
        kern,
        grid=(B, H // 2),
        in_specs=[
            pl.BlockSpec((1, 2, S, HEAD_PAD), whole),
            pl.BlockSpec((1, 2, S, HEAD_PAD), whole),
            pl.BlockSpec((1, 2, S, HEAD_PAD), whole),
        ],
        out_specs=pl.BlockSpec((1, S, 2 * V_HEAD), lambda b, hp: (b, 0, hp)),
        out_shape=jax.ShapeDtypeStruct((B, S, MLA_WIDTH), BF),
        scratch_shapes=[
            pltpu.VMEM((2, 2, tq, tk), F32),
            pltpu.VMEM((2, 2, tq, tk), BF),
            pltpu.VMEM((2, tq, LANES), F32),
            pltpu.VMEM((2, 2, tq, LANES), F32),
            pltpu.VMEM((2, tq, HEAD_PAD), F32),
        ],
        name="attn",
        compiler_params=_params(),
    )(q, k, v)


def _window_sum(ubuf, pbuf, lanes, w, tm):
    half = w // 2
    src, rows, m, slot = (lambda a, b: ubuf[a:b, lanes]), tm + 3 * HALO, 1, 0
    while m < half:
        rows -= 8
        pbuf[slot, 0:rows, :] = src(0, rows) + src(m, rows + m)
        src = functools.partial(lambda s, a, b: pbuf[s, a:b, :], slot)
        m, slot = 2 * m, 1 - slot
    return src(HALO - half, HALO - half + tm) + src(HALO, HALO + tm)


def _merge_kernel(x_ref, u_ref, up_ref, un_ref, icnt_ref, b_ref, m_ref, ln_ref, wg_ref, bg_ref, mix_ref, ps_ref,
                  wbr_ref, wo_ref, lnp_ref, o_ref, ubuf, pbuf):
    tm = x_ref.shape[1]
    i = pl.program_id(1)
    x = x_ref[0]
    h = _rms(x, ln_ref[...]).astype(BF)

    ubuf[0:HALO, :] = jnp.where(i > 0, up_ref[0], 0.0)
    ubuf[HALO:HALO + tm, :] = u_ref[0]
    ubuf[HALO + tm:2 * HALO + tm, :] = jnp.where(i < pl.num_programs(1) - 1, un_ref[0], 0.0)
    ubuf[2 * HALO + tm:, :] = jnp.zeros((HALO, POOL_WIDTH), F32)
    icnt = icnt_ref[...]
    a_parts = []
    for g, w in enumerate(POOL_WINDOWS):
        sl = slice(g * POOL_GROUP_W, (g + 1) * POOL_GROUP_W)
        tot = _window_sum(ubuf, pbuf, sl, w, tm)
        diff = (tot * icnt[:, g:g + 1] - ubuf[HALO:HALO + tm, sl]).astype(BF)
        a_parts.append(jnp.dot(diff, mix_ref[g], preferred_element_type=F32))
    a_out = (jnp.concatenate(a_parts, axis=-1) * ps_ref[...]).astype(BF)

    branches = (a_out, b_ref[0], m_ref[0])
    merged = None
    for n in range(N_BRANCH):
        sl = slice(n * D_MODEL, (n + 1) * D_MODEL)
        gate = jax.nn.sigmoid(jnp.dot(h, wg_ref[:, sl], preferred_element_type=F32) + bg_ref[:, sl])
        term = gate * jnp.dot(branches[n], wbr_ref[n], preferred_element_type=F32)
        merged = term if merged is None else merged + term
    y = jnp.dot(merged.astype(BF), wo_ref[...], preferred_element_type=F32)
    o_ref[0] = x + _rms(y, lnp_ref[...])


def _merge(x, u, b_out, m_out, ln, wg, bg, mix, ps, wbr, wo, lnp):
    B, S, _ = x.shape
    tm = TM_MERGE
    hb = tm // HALO
    n_halo = S // HALO
    row = lambda b, i: (b, i, 0)
    t = np.arange(S)
    icnt = np.stack([1.0 / (np.minimum(t + w // 2, S) - np.maximum(t - w // 2, 0)) for w in POOL_WINDOWS], axis=1)
    return pl.pallas_call(
        _merge_kernel,
        grid=(B, S // tm),
        in_specs=[
            pl.BlockSpec((1, tm, D_MODEL), row),
            pl.BlockSpec((1, tm, POOL_WIDTH), row),
            pl.BlockSpec((1, HALO, POOL_WIDTH), lambda b, i: (b, jnp.maximum(i * hb - 1, 0), 0)),
            pl.BlockSpec((1, HALO, POOL_WIDTH), lambda b, i: (b, jnp.minimum((i + 1) * hb, n_halo - 1), 0)),
            pl.BlockSpec((tm, POOL_GROUPS), lambda b, i: (i, 0)),
            pl.BlockSpec((1, tm, MLA_WIDTH), row),
            pl.BlockSpec((1, tm, X_WIDTH), row),
            _const_spec((1, D_MODEL)),
            _const_spec((D_MODEL, N_BRANCH * D_MODEL)),
            _const_spec((1, N_BRANCH * D_MODEL)),
            _const_spec((POOL_GROUPS, POOL_GROUP_W, POOL_GROUP_W)),
            _const_spec((1, POOL_WIDTH)),
            _const_spec((N_BRANCH, BRANCH_WIDTH, D_MODEL)),
            _const_spec((D_MODEL, D_MODEL)),
            _const_spec((1, D_MODEL)),
        ],
        out_specs=pl.BlockSpec((1, tm, D_MODEL), row),
        out_shape=jax.ShapeDtypeStruct((B, S, D_MODEL), F32),
        scratch_shapes=[
            pltpu.VMEM((tm + 3 * HALO, POOL_WIDTH), F32),
            pltpu.VMEM((2, tm + 3 * HALO - 8, POOL_GROUP_W), F32),
        ],
        name="merge",
        compiler_params=_params(),
    )(x, u, u, u, jnp.asarray(icnt, F32), b_out, m_out, ln, wg, bg, mix, ps, wbr, wo, lnp)


def _ffn_kernel(x_ref, ln_ref, wgu_ref, wd_ref, lnp_ref, o_ref):
    x = x_ref[...]
    h = _rms(x, ln_ref[...]).astype(BF)
    f = None
    for c in range(D_FF // FF_CHUNK):
        g = jnp.dot(h, wgu_ref[:, c * FF_CHUNK:(c + 1) * FF_CHUNK], preferred_element_type=F32)
        u = jnp.dot(h, wgu_ref[:, D_FF + c * FF_CHUNK:D_FF + (c + 1) * FF_CHUNK], preferred_element_type=F32)
        a = (g * jax.nn.sigmoid(g) * u).astype(BF)
        part = jnp.dot(a, wd_ref[c * FF_CHUNK:(c + 1) * FF_CHUNK, :], preferred_element_type=F32)
        f = part if f is None else f + part
    o_ref[...] = x + _rms(f, lnp_ref[...])


def _ffn(x, ln, wgu, wd, lnp):
    B, S, _ = x.shape
    n = B * S
    tm = TM_FFN
    out = pl.pallas_call(
        _ffn_kernel,
        grid=(n // tm,),
        in_specs=[
            pl.BlockSpec((tm, D_MODEL), lambda i: (i, 0)),
            _const_spec((1, D_MODEL)),
            _const_spec((D_MODEL, 2 * D_FF)),
            _const_spec((D_FF, D_MODEL)),
            _const_spec((1, D_MODEL)),
        ],
        out_specs=pl.BlockSpec((tm, D_MODEL), lambda i: (i, 0)),
        out_shape=jax.ShapeDtypeStruct((n, D_MODEL), F32),
        name="ffn",
        compiler_params=_params(),
    )(x.reshape(n, D_MODEL), ln, wgu, wd, lnp)
    return out.reshape(B, S, D_MODEL)


def _rope_tables(seq):
    inv = 1.0 / (ROPE_THETA ** (np.arange(0, QK_ROPE, 2, dtype=np.float32) / QK_ROPE))
    ang = np.arange(seq, dtype=np.float32)[:, None] * inv[None, :]
    cos, sin = np.cos(ang), np.sin(ang)
    ct = np.zeros((seq, LANES), np.float32)
    sa = np.zeros((seq, LANES), np.float32)
    sb = np.zeros((seq, LANES), np.float32)
    x1 = slice(QK_NOPE, QK_NOPE + ROPE_HALF)
    x2 = slice(QK_NOPE + ROPE_HALF, QK_NOPE + QK_ROPE)
    ct[:, :QK_NOPE] = 1.0
    ct[:, x1] = cos
    ct[:, x2] = cos
    sa[:, x2] = sin
    sb[:, x1] = -sin
    return jnp.asarray(ct), jnp.asarray(sa), jnp.asarray(sb)


def _head_slabs(w, width, offset):
    L, K, _ = w.shape
    w = w.reshape(L, K, MLA_HEADS, width)
    w = jnp.pad(w, ((0, 0), (0, 0), (0, 0), (offset, HEAD_PAD - width - offset)))
    return w.reshape(L, K, MLA_HEADS * HEAD_PAD)


def _prep_weights(w_in, w_uq, w_uk, w_uv):
    pool, cq, ckv, kr, qx = (w_in[..., a:b] for a, b in
                             ((0, 512), (512, 896), (896, 1152), (1152, 1184), (1184, 1696)))
    kr_slab = jnp.pad(kr, ((0, 0), (0, 0), (QK_NOPE, LANES - QK_NOPE - QK_ROPE)))
    w_in_p = jnp.concatenate([pool, cq, ckv, qx, kr_slab], axis=-1).astype(BF)
    w_uq_p = _head_slabs(w_uq, QK_NOPE + QK_ROPE, 0).astype(BF)
    w_uk_p = _head_slabs(w_uk, QK_NOPE, 0).astype(BF)
    v_even = _head_slabs(w_uv, V_HEAD, 0).reshape(DEPTH, KV_LORA, MLA_HEADS, HEAD_PAD)
    v_odd = _head_slabs(w_uv, V_HEAD, V_HEAD).reshape(DEPTH, KV_LORA, MLA_HEADS, HEAD_PAD)
    odd = (jnp.arange(MLA_HEADS) % 2 == 1)[None, None, :, None]
    w_uv_p = jnp.where(odd, v_odd, v_even).reshape(DEPTH, KV_LORA, MLA_HEADS * HEAD_PAD).astype(BF)
    vone = np.zeros((MLA_HEADS, HEAD_PAD), np.float32)
    vone[0::2, V_HEAD] = 1.0
    vone[1::2, 0] = 1.0
    return w_in_p, w_uq_p, w_uk_p, w_uv_p, jnp.asarray(vone.reshape(1, -1))


def kernel(x_prompt, x_sample, mem_prompt, mem_sample, w_in, q_norm, kv_norm, w_uq, w_uk, w_uv, pool_mix,
           pool_scale, mem_norm, w_mem_kv, w_branch, w_gate, b_gate, w_out, ln_mix_pre, ln_mix_post,
           ln_ffn_pre, ln_ffn_post, w_gu, w_down):
    w_in_p, w_uq_p, w_uk_p, w_uv_p, vone = _prep_weights(w_in, w_uq, w_uk, w_uv)
    w_mem_kv_b = w_mem_kv.astype(BF)
    w_gate_b, w_branch_b, w_out_b = w_gate.astype(BF), w_branch.astype(BF), w_out.astype(BF)
    pool_mix_b, w_gu_b, w_down_b = pool_mix.astype(BF), w_gu.astype(BF), w_down.astype(BF)
    vec = lambda a, l: a[l][None, :]

    def trunk(x, mem):
        tabs = _rope_tables(x.shape[1])
        kmem, vmem = _memkv(mem, mem_norm, w_mem_kv_b)
        for l in range(DEPTH):
            u, q, k, v, m_out = _pre(x, tabs, kmem, vmem, l, vec(ln_mix_pre, l), w_in_p[l], vec(q_norm, l),
                                     vec(kv_norm, l), w_uq_p[l], w_uk_p[l], w_uv_p[l], vone)
            b_out = _attn(q, k, v)
            x = _merge(x, u, b_out, m_out, vec(ln_mix_pre, l), w_gate_b[l], vec(b_gate, l), pool_mix_b[l],
                       vec(pool_scale, l), w_branch_b[l], w_out_b[l], vec(ln_mix_post, l))
            x = _ffn(x, vec(ln_ffn_pre, l), w_gu_b[l], w_down_b[l], vec(ln_ffn_post, l))
        return x

    return (trunk(x_prompt, mem_prompt), trunk(x_sample, mem_sample))
```

```python
import functools

import jax
import jax.numpy as jnp
import numpy as np
from jax import lax
from jax.experimental import pallas as pl
from jax.experimental.pallas import tpu as pltpu

D_MODEL = 1024
DEPTH = 4
N_MEM = 256
POOL_WINDOWS = (2, 4, 8, 16)
POOL_GROUPS = len(POOL_WINDOWS)
POOL_WIDTH = 512
POOL_GROUP_W = POOL_WIDTH // POOL_GROUPS
MLA_HEADS = 8
QK_NOPE = 64
QK_ROPE = 32
ROPE_HALF = QK_ROPE // 2
V_HEAD = 64
Q_LORA = 384
KV_LORA = 256
MLA_WIDTH = MLA_HEADS * V_HEAD
ROPE_THETA = 10000.0
X_HEADS = 4
X_HEAD_DIM = 128
X_WIDTH = X_HEADS * X_HEAD_DIM
N_BRANCH = 3
BRANCH_WIDTH = 512
D_FF = 2816
EPS = 1e-6

LANES = 128
HEAD_PAD = LANES
HALO = 16
Z_POOL = 0
Z_CQ = Z_POOL + POOL_WIDTH
Z_CKV = Z_CQ + Q_LORA
Z_QX = Z_CKV + KV_LORA
Z_KR = Z_QX + X_WIDTH
Z_WIDTH = Z_KR + LANES
FF_CHUNK = 256
LOG2E = 1.4426950408889634
VMEM_LIMIT = 56 * 1024 * 1024

TM_PRE = 1024
TM_MERGE = 1024
TM_FFN = 1024
ATT_CHUNKS = 4
ATT_MIN_TK = 1024
ATT_TILE_ELEMS = 512 * 2048
ATT_MAX_TQ = 1024
SOFTMAX_ROWS = 32

BF = jnp.bfloat16
F32 = jnp.float32


def _rms(x, g):
    y = x * lax.rsqrt(jnp.mean(x * x, axis=-1, keepdims=True) + EPS)
    return y * g


def _const_spec(shape):
    nd = len(shape)
    return pl.BlockSpec(shape, lambda *_: (0,) * nd, pipeline_mode=pl.Buffered(1))


def _params():
    return pltpu.CompilerParams(vmem_limit_bytes=VMEM_LIMIT)


def _memkv_kernel(mem_ref, g_ref, w_ref, k_ref, v_ref):
    mn = _rms(mem_ref[0], g_ref[0]).astype(BF)
    kv = jnp.dot(mn, w_ref[0], preferred_element_type=F32)
    k_ref[0, 0] = kv[:, :X_WIDTH].astype(BF)
    v_ref[0, 0] = kv[:, X_WIDTH:].astype(BF)


def _memkv(mem, mem_norm, w_mem_kv):
    B = mem.shape[0]
    out = jax.ShapeDtypeStruct((DEPTH, B, N_MEM, X_WIDTH), BF)
    return pl.pallas_call(
        _memkv_kernel,
        grid=(DEPTH, B),
        in_specs=[
            pl.BlockSpec((1, N_MEM, D_MODEL), lambda l, b: (b, 0, 0)),
            pl.BlockSpec((1, 1, D_MODEL), lambda l, b: (l, 0, 0)),
            pl.BlockSpec((1, D_MODEL, 2 * X_WIDTH), lambda l, b: (l, 0, 0)),
        ],
        out_specs=[
            pl.BlockSpec((1, 1, N_MEM, X_WIDTH), lambda l, b: (l, b, 0, 0)),
            pl.BlockSpec((1, 1, N_MEM, X_WIDTH), lambda l, b: (l, b, 0, 0)),
        ],
        out_shape=[out, out],
        name="memkv",
        compiler_params=_params(),
    )(mem, mem_norm.reshape(DEPTH, 1, D_MODEL), w_mem_kv)


def _rope_slab(x, ct, sa, sb):
    return x * ct + pltpu.roll(x, ROPE_HALF, 1) * sa + pltpu.roll(x, LANES - ROPE_HALF, 1) * sb


def _pre_kernel(x_ref, ct_ref, sa_ref, sb_ref, kmem_ref, vmem_ref, ln_ref, w_in_ref, qn_ref, kvn_ref,
                w_uq_ref, w_uk_ref, w_uv_ref, vone_ref,
                u_ref, q_ref, k_ref, v_ref, m_ref):
    h = _rms(x_ref[0], ln_ref[...]).astype(BF)
    z = jnp.dot(h, w_in_ref[...], preferred_element_type=F32)
    u_ref[0] = z[:, Z_POOL:Z_POOL + POOL_WIDTH]

    ct, sa, sb = ct_ref[...], sa_ref[...], sb_ref[...]
    q_scale = (QK_NOPE + QK_ROPE) ** -0.5 * LOG2E
    cq = _rms(z[:, Z_CQ:Z_CQ + Q_LORA], qn_ref[...]).astype(BF)
    qf = jnp.dot(cq, w_uq_ref[...], preferred_element_type=F32)
    c = _rms(z[:, Z_CKV:Z_CKV + KV_LORA], kvn_ref[...]).astype(BF)
    kf = jnp.dot(c, w_uk_ref[...], preferred_element_type=F32)
    vf = jnp.dot(c, w_uv_ref[...], preferred_element_type=F32) + vone_ref[...]
    kr = _rope_slab(z[:, Z_KR:Z_KR + LANES], ct, sa, sb)
    for hd in range(MLA_HEADS):
        sl = slice(hd * HEAD_PAD, (hd + 1) * HEAD_PAD)
        q_ref[0, hd] = (_rope_slab(qf[:, sl], ct, sa, sb) * q_scale).astype(BF)
        k_ref[0, hd] = (kf[:, sl] + kr).astype(BF)
        v_ref[0, hd] = vf[:, sl].astype(BF)

    x_scale = X_HEAD_DIM ** -0.5 * LOG2E
    qx = (z[:, Z_QX:Z_QX + X_WIDTH] * x_scale).astype(BF)
    outs = []
    for hd in range(X_HEADS):
        sl = slice(hd * X_HEAD_DIM, (hd + 1) * X_HEAD_DIM)
        s = lax.dot_general(qx[:, sl], kmem_ref[0, 0, :, sl], (((1,), (1,)), ((), ())),
                            preferred_element_type=F32)
        p = jnp.exp2(s - jnp.max(s, axis=-1, keepdims=True))
        l = jnp.sum(p, axis=-1, keepdims=True)
        o = jnp.dot(p.astype(BF), vmem_ref[0, 0, :, sl], preferred_element_type=F32)
        outs.append(o / l)
    m_ref[0] = jnp.concatenate(outs, axis=-1).astype(BF)


def _pre(x, tabs, kmem, vmem, l, ln, w_in, qn, kvn, w_uq, w_uk, w_uv, vone):
    B, S, _ = x.shape
    tm = TM_PRE
    row = lambda b, i: (b, i, 0)
    head = lambda b, i: (b, 0, i, 0)
    tab = lambda b, i: (i, 0)
    att = jax.ShapeDtypeStruct((B, MLA_HEADS, S, HEAD_PAD), BF)
    return pl.pallas_call(
        _pre_kernel,
        grid=(B, S // tm),
        in_specs=[
            pl.BlockSpec((1, tm, D_MODEL), row),
            pl.BlockSpec((tm, LANES), tab), pl.BlockSpec((tm, LANES), tab), pl.BlockSpec((tm, LANES), tab),
            pl.BlockSpec((1, 1, N_MEM, X_WIDTH), lambda b, i: (l, b, 0, 0)),
            pl.BlockSpec((1, 1, N_MEM, X_WIDTH), lambda b, i: (l, b, 0, 0)),
            _const_spec((1, D_MODEL)), _const_spec((D_MODEL, Z_WIDTH)),
            _const_spec((1, Q_LORA)), _const_spec((1, KV_LORA)),
            _const_spec((Q_LORA, MLA_HEADS * HEAD_PAD)),
            _const_spec((KV_LORA, MLA_HEADS * HEAD_PAD)),
            _const_spec((KV_LORA, MLA_HEADS * HEAD_PAD)),
            _const_spec((1, MLA_HEADS * HEAD_PAD)),
        ],
        out_specs=[
            pl.BlockSpec((1, tm, POOL_WIDTH), row),
            pl.BlockSpec((1, MLA_HEADS, tm, HEAD_PAD), head),
            pl.BlockSpec((1, MLA_HEADS, tm, HEAD_PAD), head),
            pl.BlockSpec((1, MLA_HEADS, tm, HEAD_PAD), head),
            pl.BlockSpec((1, tm, X_WIDTH), row),
        ],
        out_shape=[
            jax.ShapeDtypeStruct((B, S, POOL_WIDTH), F32), att, att, att,
            jax.ShapeDtypeStruct((B, S, X_WIDTH), BF),
        ],
        name="pre",
        compiler_params=_params(),
    )(x, *tabs, kmem, vmem, ln, w_in, qn, kvn, w_uq, w_uk, w_uv, vone)


def _attn_kernel(q_ref, k_ref, v_ref, o_ref, s_scr, p_scr, m_scr, a_scr, acc_scr, *, n_chunks, tk, tq):
    heads = range(q_ref.shape[1])
    nt = (((1,), (1,)), ((), ()))
    lax.fori_loop(0, q_ref.shape[2] // tq,
                  functools.partial(_attn_tile, q_ref, k_ref, v_ref, o_ref, s_scr, p_scr, m_scr, a_scr, acc_scr,
                                    heads, nt, n_chunks, tk, tq), 0)


def _attn_tile(q_ref, k_ref, v_ref, o_ref, s_scr, p_scr, m_scr, a_scr, acc_scr, heads, nt, n_chunks, tk, tq, t, carry):
    row0 = pl.multiple_of(t * tq, tq)

    def scores(j, c):
        s_scr[j, c % 2] = lax.dot_general(q_ref[0, j, pl.ds(row0, tq), :], k_ref[0, j, c * tk:(c + 1) * tk, :], nt,
                                          preferred_element_type=F32)

    def softmax(j, c):
        slot = c % 2
        for r in range(tq // SOFTMAX_ROWS):
            rows = slice(r * SOFTMAX_ROWS, (r + 1) * SOFTMAX_ROWS)
            mx = s_scr[j, slot, rows, 0:LANES]
            for kb in range(1, tk // LANES):
                mx = jnp.maximum(mx, s_scr[j, slot, rows, kb * LANES:(kb + 1) * LANES])
            mx = jnp.broadcast_to(jnp.max(mx, axis=-1, keepdims=True), (SOFTMAX_ROWS, LANES))
            if c == 0:
                m_new = mx
            else:
                m_old = m_scr[j, rows, :]
                m_new = jnp.maximum(m_old, mx)
                a_scr[j, slot, rows, :] = jnp.exp2(m_old - m_new)
            for kb in range(tk // LANES):
                cols = slice(kb * LANES, (kb + 1) * LANES)
                p_scr[j, slot, rows, cols] = jnp.exp2(s_scr[j, slot, rows, cols] - m_new).astype(BF)
            m_scr[j, rows, :] = m_new

    def values(j, c):
        slot = c % 2
        pv = jnp.dot(p_scr[j, slot], v_ref[0, j, c * tk:(c + 1) * tk, :], preferred_element_type=F32)
        acc_scr[j] = pv if c == 0 else acc_scr[j] * a_scr[j, slot] + pv

    for j in heads:
        scores(j, 0)
    for c in range(n_chunks):
        for j in heads:
            if c + 1 < n_chunks:
                scores(j, c + 1)
            softmax(j, c)
            if c >= 1:
                values(j, c - 1)
    for j in heads:
        values(j, n_chunks - 1)
    o0 = acc_scr[0] / acc_scr[0][:, V_HEAD:V_HEAD + 1]
    o1 = acc_scr[1] / acc_scr[1][:, 0:1]
    lane = lax.broadcasted_iota(jnp.int32, (tq, HEAD_PAD), 1)
    o_ref[0, pl.ds(row0, tq), :] = jnp.where(lane < V_HEAD, o0, o1).astype(BF)
    return carry


def _attn(q, k, v):
    B, H, S, _ = q.shape
    tk = max(S // ATT_CHUNKS, ATT_MIN_TK)
    tq = min(ATT_TILE_ELEMS // tk, ATT_MAX_TQ)
    assert S % tk == 0 and tk % LANES == 0 and S % tq == 0 and tq % SOFTMAX_ROWS == 0
    kern = functools.partial(_attn_kernel, n_chunks=S // tk, tk=tk, tq=tq)
    whole = lambda b, hp: (b, hp, 0, 0)
    return pl.pallas_call(
        kern,
        grid=(B, H // 2),
        in_specs=[
            pl.BlockSpec((1, 2, S, HEAD_PAD), whole),
            pl.BlockSpec((1, 2, S, HEAD_PAD), whole),
            pl.BlockSpec((1, 2, S, HEAD_PAD), whole),
        ],
        out_specs=pl.BlockSpec((1, S, 2 * V_HEAD), lambda b, hp: (b, 0, hp)),
        out_shape=jax.ShapeDtypeStruct((B, S, MLA_WIDTH), BF),
        scratch_shapes=[
            pltpu.VMEM((2, 2, tq, tk), F32),
            pltpu.VMEM((2, 2, tq, tk), BF),
            pltpu.VMEM((2, tq, LANES), F32),
            pltpu.VMEM((2, 2, tq, LANES), F32),
            pltpu.VMEM((2, tq, HEAD_PAD), F32),
        ],
        name="attn",
        compiler_params=_params(),
    )(q, k, v)


def _window_sum(ubuf, pbuf, lanes, w, tm):
    half = w // 2
    src, rows, m, slot = (lambda a, b: ubuf[a:b, lanes]), tm + 3 * HALO, 1, 0
    while m < half:
        rows -= 8
        pbuf[slot, 0:rows, :] = src(0, rows) + src(m, rows + m)
        src = functools.partial(lambda s, a, b: pbuf[s, a:b, :], slot)
        m, slot = 2 * m, 1 - slot
    return src(HALO - half, HALO - half + tm) + src(HALO, HALO + tm)


def _merge_kernel(x_ref, u_ref, up_ref, un_ref, icnt_ref, b_ref, m_ref, ln_ref, wg_ref, bg_ref, mix_ref, ps_ref,
                  wbr_ref, wo_ref, lnp_ref, o_ref, ubuf, pbuf):
    tm = x_ref.shape[1]
    i = pl.program_id(1)
    x = x_ref[0]
    h = _rms(x, ln_ref[...]).astype(BF)

    def gate(n):
        sl = slice(n * D_MODEL, (n + 1) * D_MODEL)
        return jax.nn.sigmoid(jnp.dot(h, wg_ref[:, sl], preferred_element_type=F32) + bg_ref[:, sl])

    merged = (gate(1) * jnp.dot(b_ref[0], wbr_ref[1], preferred_element_type=F32)
              + gate(2) * jnp.dot(m_ref[0], wbr_ref[2], preferred_element_type=F32))
    gate_pool = gate(0)

    ubuf[0:HALO, :] = jnp.where(i > 0, up_ref[0], 0.0)
    ubuf[HALO:HALO + tm, :] = u_ref[0]
    ubuf[HALO + tm:2 * HALO + tm, :] = jnp.where(i < pl.num_programs(1) - 1, un_ref[0], 0.0)
    ubuf[2 * HALO + tm:, :] = jnp.zeros((HALO, POOL_WIDTH), F32)
    icnt = icnt_ref[...]
    a_parts = []
    for g, w in enumerate(POOL_WINDOWS):
        sl = slice(g * POOL_GROUP_W, (g + 1) * POOL_GROUP_W)
        tot = _window_sum(ubuf, pbuf, sl, w, tm)
        diff = (tot * icnt[:, g:g + 1] - ubuf[HALO:HALO + tm, sl]).astype(BF)
        a_parts.append(jnp.dot(diff, mix_ref[g], preferred_element_type=F32))
    a_out = (jnp.concatenate(a_parts, axis=-1) * ps_ref[...]).astype(BF)

    merged = merged + gate_pool * jnp.dot(a_out, wbr_ref[0], preferred_element_type=F32)
    y = jnp.dot(merged.astype(BF), wo_ref[...], preferred_element_type=F32)
    o_ref[0] = x + _rms(y, lnp_ref[...])


def _merge(x, u, b_out, m_out, ln, wg, bg, mix, ps, wbr, wo, lnp):
    B, S, _ = x.shape
    tm = TM_MERGE
    hb = tm // HALO
    n_halo = S // HALO
    row = lambda b, i: (b, i, 0)
    t = np.arange(S)
    icnt = np.stack([1.0 / (np.minimum(t + w // 2, S) - np.maximum(t - w // 2, 0)) for w in POOL_WINDOWS], axis=1)
    return pl.pallas_call(
        _merge_kernel,
        grid=(B, S // tm),
        in_specs=[
            pl.BlockSpec((1, tm, D_MODEL), row),
            pl.BlockSpec((1, tm, POOL_WIDTH), row),
            pl.BlockSpec((1, HALO, POOL_WIDTH), lambda b, i: (b, jnp.maximum(i * hb - 1, 0), 0)),
            pl.BlockSpec((1, HALO, POOL_WIDTH), lambda b, i: (b, jnp.minimum((i + 1) * hb, n_halo - 1), 0)),
            pl.BlockSpec((tm, POOL_GROUPS), lambda b, i: (i, 0)),
            pl.BlockSpec((1, tm, MLA_WIDTH), row),
            pl.BlockSpec((1, tm, X_WIDTH), row),
            _const_spec((1, D_MODEL)),
            _const_spec((D_MODEL, N_BRANCH * D_MODEL)),
            _const_spec((1, N_BRANCH * D_MODEL)),
            _const_spec((POOL_GROUPS, POOL_GROUP_W, POOL_GROUP_W)),
            _const_spec((1, POOL_WIDTH)),
            _const_spec((N_BRANCH, BRANCH_WIDTH, D_MODEL)),
            _const_spec((D_MODEL, D_MODEL)),
            _const_spec((1, D_MODEL)),
        ],
        out_specs=pl.BlockSpec((1, tm, D_MODEL), row),
        out_shape=jax.ShapeDtypeStruct((B, S, D_MODEL), F32),
        scratch_shapes=[
            pltpu.VMEM((tm + 3 * HALO, POOL_WIDTH), F32),
            pltpu.VMEM((2, tm + 3 * HALO - 8, POOL_GROUP_W), F32),
        ],
        name="merge",
        compiler_params=_params(),
    )(x, u, u, u, jnp.asarray(icnt, F32), b_out, m_out, ln, wg, bg, mix, ps, wbr, wo, lnp)


def _ffn_kernel(x_ref, ln_ref, wgu_ref, wd_ref, lnp_ref, o_ref):
    half = x_ref.shape[0] // 2
    for r in range(2):
        rows = slice(r * half, (r + 1) * half)
        x = x_ref[rows, :]
        h = _rms(x, ln_ref[...]).astype(BF)
        f = None
        for c in range(D_FF // FF_CHUNK):
            g = jnp.dot(h, wgu_ref[:, c * FF_CHUNK:(c + 1) * FF_CHUNK], preferred_element_type=F32)
            u = jnp.dot(h, wgu_ref[:, D_FF + c * FF_CHUNK:D_FF + (c + 1) * FF_CHUNK], preferred_element_type=F32)
            a = (g * jax.nn.sigmoid(g) * u).astype(BF)
            part = jnp.dot(a, wd_ref[c * FF_CHUNK:(c + 1) * FF_CHUNK, :], preferred_element_type=F32)
            f = part if f is None else f + part
        o_ref[rows, :] = x + _rms(f, lnp_ref[...])


def _ffn(x, ln, wgu, wd, lnp):
    B, S, _ = x.shape
    n = B * S
    tm = TM_FFN
    out = pl.pallas_call(
        _ffn_kernel,
        grid=(n // tm,),
        in_specs=[
            pl.BlockSpec((tm, D_MODEL), lambda i: (i, 0)),
            _const_spec((1, D_MODEL)),
            _const_spec((D_MODEL, 2 * D_FF)),
            _const_spec((D_FF, D_MODEL)),
            _const_spec((1, D_MODEL)),
        ],
        out_specs=pl.BlockSpec((tm, D_MODEL), lambda i: (i, 0)),
        out_shape=jax.ShapeDtypeStruct((n, D_MODEL), F32),
        name="ffn",
        compiler_params=_params(),
    )(x.reshape(n, D_MODEL), ln, wgu, wd, lnp)
    return out.reshape(B, S, D_MODEL)


def _rope_tables(seq):
    inv = 1.0 / (ROPE_THETA ** (np.arange(0, QK_ROPE, 2, dtype=np.float32) / QK_ROPE))
    ang = np.arange(seq, dtype=np.float32)[:, None] * inv[None, :]
    cos, sin = np.cos(ang), np.sin(ang)
    ct = np.zeros((seq, LANES), np.float32)
    sa = np.zeros((seq, LANES), np.float32)
    sb = np.zeros((seq, LANES), np.float32)
    x1 = slice(QK_NOPE, QK_NOPE + ROPE_HALF)
    x2 = slice(QK_NOPE + ROPE_HALF, QK_NOPE + QK_ROPE)
    ct[:, :QK_NOPE] = 1.0
    ct[:, x1] = cos
    ct[:, x2] = cos
    sa[:, x2] = sin
    sb[:, x1] = -sin
    return jnp.asarray(ct), jnp.asarray(sa), jnp.asarray(sb)


def _head_slabs(w, width, offset):
    L, K, _ = w.shape
    w = w.reshape(L, K, MLA_HEADS, width)
    w = jnp.pad(w, ((0, 0), (0, 0), (0, 0), (offset, HEAD_PAD - width - offset)))
    return w.reshape(L, K, MLA_HEADS * HEAD_PAD)


def _prep_weights(w_in, w_uq, w_uk, w_uv):
    pool, cq, ckv, kr, qx = (w_in[..., a:b] for a, b in
                             ((0, 512), (512, 896), (896, 1152), (1152, 1184), (1184, 1696)))
    kr_slab = jnp.pad(kr, ((0, 0), (0, 0), (QK_NOPE, LANES - QK_NOPE - QK_ROPE)))
    w_in_p = jnp.concatenate([pool, cq, ckv, qx, kr_slab], axis=-1).astype(BF)
    w_uq_p = _head_slabs(w_uq, QK_NOPE + QK_ROPE, 0).astype(BF)
    w_uk_p = _head_slabs(w_uk, QK_NOPE, 0).astype(BF)
    v_even = _head_slabs(w_uv, V_HEAD, 0).reshape(DEPTH, KV_LORA, MLA_HEADS, HEAD_PAD)
    v_odd = _head_slabs(w_uv, V_HEAD, V_HEAD).reshape(DEPTH, KV_LORA, MLA_HEADS, HEAD_PAD)
    odd = (jnp.arange(MLA_HEADS) % 2 == 1)[None, None, :, None]
    w_uv_p = jnp.where(odd, v_odd, v_even).reshape(DEPTH, KV_LORA, MLA_HEADS * HEAD_PAD).astype(BF)
    vone = np.zeros((MLA_HEADS, HEAD_PAD), np.float32)
    vone[0::2, V_HEAD] = 1.0
    vone[1::2, 0] = 1.0
    return w_in_p, w_uq_p, w_uk_p, w_uv_p, jnp.asarray(vone.reshape(1, -1))


def kernel(x_prompt, x_sample, mem_prompt, mem_sample, w_in, q_norm, kv_norm, w_uq, w_uk, w_uv, pool_mix,
           pool_scale, mem_norm, w_mem_kv, w_branch, w_gate, b_gate, w_out, ln_mix_pre, ln_mix_post,
           ln_ffn_pre, ln_ffn_post, w_gu, w_down):
    w_in_p, w_uq_p, w_uk_p, w_uv_p, vone = _prep_weights(w_in, w_uq, w_uk, w_uv)
    w_mem_kv_b = w_mem_kv.astype(BF)
    w_gate_b, w_branch_b, w_out_b = w_gate.astype(BF), w_branch.astype(BF), w_out.astype(BF)
    pool_mix_b, w_gu_b, w_down_b = pool_mix.astype(BF), w_gu.astype(BF), w_down.astype(BF)
    vec = lambda a, l: a[l][None, :]

    def trunk(x, mem):
        tabs = _rope_tables(x.shape[1])
        kmem, vmem = _memkv(mem, mem_norm, w_mem_kv_b)
        for l in range(DEPTH):
            u, q, k, v, m_out = _pre(x, tabs, kmem, vmem, l, vec(ln_mix_pre, l), w_in_p[l], vec(q_norm, l),
                                     vec(kv_norm, l), w_uq_p[l], w_uk_p[l], w_uv_p[l], vone)
            b_out = _attn(q, k, v)
            x = _merge(x, u, b_out, m_out, vec(ln_mix_pre, l), w_gate_b[l], vec(b_gate, l), pool_mix_b[l],
                       vec(pool_scale, l), w_branch_b[l], w_out_b[l], vec(ln_mix_post, l))
            x = _ffn(x, vec(ln_ffn_pre, l), w_gu_b[l], w_down_b[l], vec(ln_ffn_post, l))
        return x

    return (trunk(x_prompt, mem_prompt), trunk(x_sample, mem_sample))
```

```python
import functools

import jax
import jax.numpy as jnp
import numpy as np
from jax import lax
from jax.experimental import pallas as pl
from jax.experimental.pallas import tpu as pltpu

D_MODEL = 1024
DEPTH = 4
N_MEM = 256
POOL_WINDOWS = (2, 4, 8, 16)
POOL_GROUPS = len(POOL_WINDOWS)
POOL_WIDTH = 512
POOL_GROUP_W = POOL_WIDTH // POOL_GROUPS
MLA_HEADS = 8
QK_NOPE = 64
QK_ROPE = 32
ROPE_HALF = QK_ROPE // 2
V_HEAD = 64
Q_LORA = 384
KV_LORA = 256
MLA_WIDTH = MLA_HEADS * V_HEAD
ROPE_THETA = 10000.0
X_HEADS = 4
X_HEAD_DIM = 128
X_WIDTH = X_HEADS * X_HEAD_DIM
N_BRANCH = 3
BRANCH_WIDTH = 512
D_FF = 2816
EPS = 1e-6

LANES = 128
HEAD_PAD = LANES
HALO = 16
Z_POOL = 0
Z_CQ = Z_POOL + POOL_WIDTH
Z_CKV = Z_CQ + Q_LORA
Z_QX = Z_CKV + KV_LORA
Z_KR = Z_QX + X_WIDTH
Z_WIDTH = Z_KR + LANES
FF_CHUNK = 256
LOG2E = 1.4426950408889634
VMEM_LIMIT = 56 * 1024 * 1024

TM_PRE = 1024
TM_MERGE = 1024
TM_FFN = 1024
ATT_CHUNKS = 4
ATT_MIN_TK = 1024
ATT_TILE_ELEMS = 512 * 2048
ATT_MAX_TQ = 1024
SOFTMAX_ROWS = 32

BF = jnp.bfloat16
F32 = jnp.float32


def _rms(x, g):
    y = x * lax.rsqrt(jnp.mean(x * x, axis=-1, keepdims=True) + EPS)
    return y * g


def _const_spec(shape, layer=None):
    nd = len(shape)
    if layer is None:
        return pl.BlockSpec(shape, lambda *_: (0,) * nd, pipeline_mode=pl.Buffered(1))
    return pl.BlockSpec((None,) + tuple(shape), lambda *_: (layer,) + (0,) * nd, pipeline_mode=pl.Buffered(1))


def _params():
    return pltpu.CompilerParams(vmem_limit_bytes=VMEM_LIMIT)


def _memkv_kernel(mem_ref, g_ref, w_ref, k_ref, v_ref):
    mn = _rms(mem_ref[0], g_ref[0]).astype(BF)
    kv = jnp.dot(mn, w_ref[0], preferred_element_type=F32)
    k_ref[0, 0] = kv[:, :X_WIDTH].astype(BF)
    v_ref[0, 0] = kv[:, X_WIDTH:].astype(BF)


def _memkv(mem, mem_norm, w_mem_kv):
    B = mem.shape[0]
    out = jax.ShapeDtypeStruct((DEPTH, B, N_MEM, X_WIDTH), BF)
    return pl.pallas_call(
        _memkv_kernel,
        grid=(DEPTH, B),
        in_specs=[
            pl.BlockSpec((1, N_MEM, D_MODEL), lambda l, b: (b, 0, 0)),
            pl.BlockSpec((1, 1, D_MODEL), lambda l, b: (l, 0, 0)),
            pl.BlockSpec((1, D_MODEL, 2 * X_WIDTH), lambda l, b: (l, 0, 0)),
        ],
        out_specs=[
            pl.BlockSpec((1, 1, N_MEM, X_WIDTH), lambda l, b: (l, b, 0, 0)),
            pl.BlockSpec((1, 1, N_MEM, X_WIDTH), lambda l, b: (l, b, 0, 0)),
        ],
        out_shape=[out, out],
        name="memkv",
        compiler_params=_params(),
    )(mem, mem_norm.reshape(DEPTH, 1, D_MODEL), w_mem_kv)


def _rope_slab(x, ct, sa, sb):
    return x * ct + pltpu.roll(x, ROPE_HALF, 1) * sa + pltpu.roll(x, LANES - ROPE_HALF, 1) * sb


def _pre_kernel(x_ref, ct_ref, sa_ref, sb_ref, kmem_ref, vmem_ref, ln_ref, w_in_ref, qn_ref, kvn_ref,
                w_uq_ref, w_uk_ref, w_uv_ref, vone_ref,
                u_ref, q_ref, k_ref, v_ref, m_ref):
    h = _rms(x_ref[0], ln_ref[...]).astype(BF)
    z = jnp.dot(h, w_in_ref[...], preferred_element_type=F32)
    u_ref[0] = z[:, Z_POOL:Z_POOL + POOL_WIDTH]

    ct, sa, sb = ct_ref[...], sa_ref[...], sb_ref[...]
    q_scale = (QK_NOPE + QK_ROPE) ** -0.5 * LOG2E
    cq = _rms(z[:, Z_CQ:Z_CQ + Q_LORA], qn_ref[...]).astype(BF)
    qf = jnp.dot(cq, w_uq_ref[...], preferred_element_type=F32)
    c = _rms(z[:, Z_CKV:Z_CKV + KV_LORA], kvn_ref[...]).astype(BF)
    kf = jnp.dot(c, w_uk_ref[...], preferred_element_type=F32)
    vf = jnp.dot(c, w_uv_ref[...], preferred_element_type=F32) + vone_ref[...]
    kr = _rope_slab(z[:, Z_KR:Z_KR + LANES], ct, sa, sb)
    for hd in range(MLA_HEADS):
        sl = slice(hd * HEAD_PAD, (hd + 1) * HEAD_PAD)
        q_ref[0, hd] = (_rope_slab(qf[:, sl], ct, sa, sb) * q_scale).astype(BF)
        k_ref[0, hd] = (kf[:, sl] + kr).astype(BF)
        v_ref[0, hd] = vf[:, sl].astype(BF)

    x_scale = X_HEAD_DIM ** -0.5 * LOG2E
    qx = (z[:, Z_QX:Z_QX + X_WIDTH] * x_scale).astype(BF)
    outs = []
    for hd in range(X_HEADS):
        sl = slice(hd * X_HEAD_DIM, (hd + 1) * X_HEAD_DIM)
        s = lax.dot_general(qx[:, sl], kmem_ref[0, 0, :, sl], (((1,), (1,)), ((), ())),
                            preferred_element_type=F32)
        p = jnp.exp2(s - jnp.max(s, axis=-1, keepdims=True))
        l = jnp.sum(p, axis=-1, keepdims=True)
        o = jnp.dot(p.astype(BF), vmem_ref[0, 0, :, sl], preferred_element_type=F32)
        outs.append(o / l)
    m_ref[0] = jnp.concatenate(outs, axis=-1).astype(BF)


def _pre(x, tabs, kmem, vmem, l, ln, w_in, qn, kvn, w_uq, w_uk, w_uv, vone):
    B, S, _ = x.shape
    tm = TM_PRE
    row = lambda b, i: (b, i, 0)
    head = lambda b, i: (b, 0, i, 0)
    tab = lambda b, i: (i, 0)
    att = jax.ShapeDtypeStruct((B, MLA_HEADS, S, HEAD_PAD), BF)
    return pl.pallas_call(
        _pre_kernel,
        grid=(B, S // tm),
        in_specs=[
            pl.BlockSpec((1, tm, D_MODEL), row),
            pl.BlockSpec((tm, LANES), tab), pl.BlockSpec((tm, LANES), tab), pl.BlockSpec((tm, LANES), tab),
            pl.BlockSpec((1, 1, N_MEM, X_WIDTH), lambda b, i: (l, b, 0, 0)),
            pl.BlockSpec((1, 1, N_MEM, X_WIDTH), lambda b, i: (l, b, 0, 0)),
            _const_spec((1, D_MODEL), l), _const_spec((D_MODEL, Z_WIDTH), l),
            _const_spec((1, Q_LORA), l), _const_spec((1, KV_LORA), l),
            _const_spec((Q_LORA, MLA_HEADS * HEAD_PAD), l),
            _const_spec((KV_LORA, MLA_HEADS * HEAD_PAD), l),
            _const_spec((KV_LORA, MLA_HEADS * HEAD_PAD), l),
            _const_spec((1, MLA_HEADS * HEAD_PAD)),
        ],
        out_specs=[
            pl.BlockSpec((1, tm, POOL_WIDTH), row),
            pl.BlockSpec((1, MLA_HEADS, tm, HEAD_PAD), head),
            pl.BlockSpec((1, MLA_HEADS, tm, HEAD_PAD), head),
            pl.BlockSpec((1, MLA_HEADS, tm, HEAD_PAD), head),
            pl.BlockSpec((1, tm, X_WIDTH), row),
        ],
        out_shape=[
            jax.ShapeDtypeStruct((B, S, POOL_WIDTH), F32), att, att, att,
            jax.ShapeDtypeStruct((B, S, X_WIDTH), BF),
        ],
        name="pre",
        compiler_params=_params(),
    )(x, *tabs, kmem, vmem, ln, w_in, qn, kvn, w_uq, w_uk, w_uv, vone)


def _attn_kernel(q_ref, k_ref, v_ref, o_ref, s_scr, p_scr, m_scr, a_scr, acc_scr, *, n_chunks, tk, tq):
    heads = range(q_ref.shape[1])
    nt = (((1,), (1,)), ((), ()))
    lax.fori_loop(0, q_ref.shape[2] // tq,
                  functools.partial(_attn_tile, q_ref, k_ref, v_ref, o_ref, s_scr, p_scr, m_scr, a_scr, acc_scr,
                                    heads, nt, n_chunks, tk, tq), 0)


def _attn_tile(q_ref, k_ref, v_ref, o_ref, s_scr, p_scr, m_scr, a_scr, acc_scr, heads, nt, n_chunks, tk, tq, t, carry):
    row0 = pl.multiple_of(t * tq, tq)

    def scores(j, c):
        s_scr[j, c % 2] = lax.dot_general(q_ref[0, j, pl.ds(row0, tq), :], k_ref[0, j, c * tk:(c + 1) * tk, :], nt,
                                          preferred_element_type=F32)

    def softmax(j, c):
        slot = c % 2
        for r in range(tq // SOFTMAX_ROWS):
            rows = slice(r * SOFTMAX_ROWS, (r + 1) * SOFTMAX_ROWS)
            mx = s_scr[j, slot, rows, 0:LANES]
            for kb in range(1, tk // LANES):
                mx = jnp.maximum(mx, s_scr[j, slot, rows, kb * LANES:(kb + 1) * LANES])
            mx = jnp.broadcast_to(jnp.max(mx, axis=-1, keepdims=True), (SOFTMAX_ROWS, LANES))
            if c == 0:
                m_new = mx
            else:
                m_old = m_scr[j, rows, :]
                m_new = jnp.maximum(m_old, mx)
                a_scr[j, slot, rows, :] = jnp.exp2(m_old - m_new)
            for kb in range(tk // LANES):
                cols = slice(kb * LANES, (kb + 1) * LANES)
                p_scr[j, slot, rows, cols] = jnp.exp2(s_scr[j, slot, rows, cols] - m_new).astype(BF)
            m_scr[j, rows, :] = m_new

    def values(j, c):
        slot = c % 2
        pv = jnp.dot(p_scr[j, slot], v_ref[0, j, c * tk:(c + 1) * tk, :], preferred_element_type=F32)
        acc_scr[j] = pv if c == 0 else acc_scr[j] * a_scr[j, slot] + pv

    for j in heads:
        scores(j, 0)
    for c in range(n_chunks):
        for j in heads:
            if c + 1 < n_chunks:
                scores(j, c + 1)
            softmax(j, c)
            if c >= 1:
                values(j, c - 1)
    for j in heads:
        values(j, n_chunks - 1)
    o0 = acc_scr[0] / acc_scr[0][:, V_HEAD:V_HEAD + 1]
    o1 = acc_scr[1] / acc_scr[1][:, 0:1]
    lane = lax.broadcasted_iota(jnp.int32, (tq, HEAD_PAD), 1)
    o_ref[0, pl.ds(row0, tq), :] = jnp.where(lane < V_HEAD, o0, o1).astype(BF)
    return carry


def _attn(q, k, v):
    B, H, S, _ = q.shape
    tk = max(S // ATT_CHUNKS, ATT_MIN_TK)
    tq = min(ATT_TILE_ELEMS // tk, ATT_MAX_TQ)
    assert S % tk == 0 and tk % LANES == 0 and S % tq == 0 and tq % SOFTMAX_ROWS == 0
    kern = functools.partial(_attn_kernel, n_chunks=S // tk, tk=tk, tq=tq)
    whole = lambda b, hp: (b, hp, 0, 0)
    return pl.pallas_call(
        kern,
        grid=(B, H // 2),
        in_specs=[
            pl.BlockSpec((1, 2, S, HEAD_PAD), whole),
            pl.BlockSpec((1, 2, S, HEAD_PAD), whole),
            pl.BlockSpec((1, 2, S, HEAD_PAD), whole),
        ],
        out_specs=pl.BlockSpec((1, S, 2 * V_HEAD), lambda b, hp: (b, 0, hp)),
        out_shape=jax.ShapeDtypeStruct((B, S, MLA_WIDTH), BF),
        scratch_shapes=[
            pltpu.VMEM((2, 2, tq, tk), F32),
            pltpu.VMEM((2, 2, tq, tk), BF),
            pltpu.VMEM((2, tq, LANES), F32),
            pltpu.VMEM((2, 2, tq, LANES), F32),
            pltpu.VMEM((2, tq, HEAD_PAD), F32),
        ],
        name="attn",
        compiler_params=_params(),
    )(q, k, v)


def _window_sum(ubuf, pbuf, lanes, w, tm):
    half = w // 2
    src, rows, m, slot = (lambda a, b: ubuf[a:b, lanes]), tm + 3 * HALO, 1, 0
    while m < half:
        rows -= 8
        pbuf[slot, 0:rows, :] = src(0, rows) + src(m, rows + m)
        src = functools.partial(lambda s, a, b: pbuf[s, a:b, :], slot)
        m, slot = 2 * m, 1 - slot
    return src(HALO - half, HALO - half + tm) + src(HALO, HALO + tm)


def _merge_kernel(x_ref, u_ref, up_ref, un_ref, icnt_ref, b_ref, m_ref, ln_ref, wg_ref, bg_ref, mix_ref, ps_ref,
                  wbr_ref, wo_ref, lnp_ref, o_ref, ubuf, pbuf):
    tm = x_ref.shape[1]
    i = pl.program_id(1)
    x = x_ref[0]
    h = _rms(x, ln_ref[...]).astype(BF)

    def gate(n):
        sl = slice(n * D_MODEL, (n + 1) * D_MODEL)
        return jax.nn.sigmoid(jnp.dot(h, wg_ref[:, sl], preferred_element_type=F32) + bg_ref[:, sl])

    merged = (gate(1) * jnp.dot(b_ref[0], wbr_ref[1], preferred_element_type=F32)
              + gate(2) * jnp.dot(m_ref[0], wbr_ref[2], preferred_element_type=F32))
    gate_pool = gate(0)

    ubuf[0:HALO, :] = jnp.where(i > 0, up_ref[0], 0.0)
    ubuf[HALO:HALO + tm, :] = u_ref[0]
    ubuf[HALO + tm:2 * HALO + tm, :] = jnp.where(i < pl.num_programs(1) - 1, un_ref[0], 0.0)
    ubuf[2 * HALO + tm:, :] = jnp.zeros((HALO, POOL_WIDTH), F32)
    icnt = icnt_ref[...]
    a_parts = []
    for g, w in enumerate(POOL_WINDOWS):
        sl = slice(g * POOL_GROUP_W, (g + 1) * POOL_GROUP_W)
        tot = _window_sum(ubuf, pbuf, sl, w, tm)
        diff = (tot * icnt[:, g:g + 1] - ubuf[HALO:HALO + tm, sl]).astype(BF)
        a_parts.append(jnp.dot(diff, mix_ref[g], preferred_element_type=F32))
    a_out = (jnp.concatenate(a_parts, axis=-1) * ps_ref[...]).astype(BF)

    merged = merged + gate_pool * jnp.dot(a_out, wbr_ref[0], preferred_element_type=F32)
    merged = merged.astype(BF)
    half = tm // 2
    for r in range(2):
        rows = slice(r * half, (r + 1) * half)
        y = jnp.dot(merged[rows, :], wo_ref[...], preferred_element_type=F32)
        o_ref[0, rows, :] = x[rows, :] + _rms(y, lnp_ref[...])


def _merge(x, u, b_out, m_out, l, ln, wg, bg, mix, ps, wbr, wo, lnp):
    B, S, _ = x.shape
    tm = TM_MERGE
    hb = tm // HALO
    n_halo = S // HALO
    row = lambda b, i: (b, i, 0)
    t = np.arange(S)
    icnt = np.stack([1.0 / (np.minimum(t + w // 2, S) - np.maximum(t - w // 2, 0)) for w in POOL_WINDOWS], axis=1)
    return pl.pallas_call(
        _merge_kernel,
        grid=(B, S // tm),
        in_specs=[
            pl.BlockSpec((1, tm, D_MODEL), row),
            pl.BlockSpec((1, tm, POOL_WIDTH), row),
            pl.BlockSpec((1, HALO, POOL_WIDTH), lambda b, i: (b, jnp.maximum(i * hb - 1, 0), 0)),
            pl.BlockSpec((1, HALO, POOL_WIDTH), lambda b, i: (b, jnp.minimum((i + 1) * hb, n_halo - 1), 0)),
            pl.BlockSpec((tm, POOL_GROUPS), lambda b, i: (i, 0)),
            pl.BlockSpec((1, tm, MLA_WIDTH), row),
            pl.BlockSpec((1, tm, X_WIDTH), row),
            _const_spec((1, D_MODEL), l),
            _const_spec((D_MODEL, N_BRANCH * D_MODEL), l),
            _const_spec((1, N_BRANCH * D_MODEL), l),
            _const_spec((POOL_GROUPS, POOL_GROUP_W, POOL_GROUP_W), l),
            _const_spec((1, POOL_WIDTH), l),
            _const_spec((N_BRANCH, BRANCH_WIDTH, D_MODEL), l),
            _const_spec((D_MODEL, D_MODEL), l),
            _const_spec((1, D_MODEL), l),
        ],
        out_specs=pl.BlockSpec((1, tm, D_MODEL), row),
        out_shape=jax.ShapeDtypeStruct((B, S, D_MODEL), F32),
        scratch_shapes=[
            pltpu.VMEM((tm + 3 * HALO, POOL_WIDTH), F32),
            pltpu.VMEM((2, tm + 3 * HALO - 8, POOL_GROUP_W), F32),
        ],
        name="merge",
        compiler_params=_params(),
    )(x, u, u, u, jnp.asarray(icnt, F32), b_out, m_out, ln, wg, bg, mix, ps, wbr, wo, lnp)


def _ffn_kernel(x_ref, ln_ref, wgu_ref, wd_ref, lnp_ref, o_ref):
    half = x_ref.shape[0] // 2
    for r in range(2):
        rows = slice(r * half, (r + 1) * half)
        x = x_ref[rows, :]
        h = _rms(x, ln_ref[...]).astype(BF)
        f = None
        for c in range(D_FF // FF_CHUNK):
            g = jnp.dot(h, wgu_ref[:, c * FF_CHUNK:(c + 1) * FF_CHUNK], preferred_element_type=F32)
            u = jnp.dot(h, wgu_ref[:, D_FF + c * FF_CHUNK:D_FF + (c + 1) * FF_CHUNK], preferred_element_type=F32)
            a = (g * jax.nn.sigmoid(g) * u).astype(BF)
            part = jnp.dot(a, wd_ref[c * FF_CHUNK:(c + 1) * FF_CHUNK, :], preferred_element_type=F32)
            f = part if f is None else f + part
        o_ref[rows, :] = x + _rms(f, lnp_ref[...])


def _ffn(x, l, ln, wgu, wd, lnp):
    B, S, _ = x.shape
    n = B * S
    tm = TM_FFN
    out = pl.pallas_call(
        _ffn_kernel,
        grid=(n // tm,),
        in_specs=[
            pl.BlockSpec((tm, D_MODEL), lambda i: (i, 0)),
            _const_spec((1, D_MODEL), l),
            _const_spec((D_MODEL, 2 * D_FF), l),
            _const_spec((D_FF, D_MODEL), l),
            _const_spec((1, D_MODEL), l),
        ],
        out_specs=pl.BlockSpec((tm, D_MODEL), lambda i: (i, 0)),
        out_shape=jax.ShapeDtypeStruct((n, D_MODEL), F32),
        name="ffn",
        compiler_params=_params(),
    )(x.reshape(n, D_MODEL), ln, wgu, wd, lnp)
    return out.reshape(B, S, D_MODEL)


def _rope_tables(seq):
    inv = 1.0 / (ROPE_THETA ** (np.arange(0, QK_ROPE, 2, dtype=np.float32) / QK_ROPE))
    ang = np.arange(seq, dtype=np.float32)[:, None] * inv[None, :]
    cos, sin = np.cos(ang), np.sin(ang)
    ct = np.zeros((seq, LANES), np.float32)
    sa = np.zeros((seq, LANES), np.float32)
    sb = np.zeros((seq, LANES), np.float32)
    x1 = slice(QK_NOPE, QK_NOPE + ROPE_HALF)
    x2 = slice(QK_NOPE + ROPE_HALF, QK_NOPE + QK_ROPE)
    ct[:, :QK_NOPE] = 1.0
    ct[:, x1] = cos
    ct[:, x2] = cos
    sa[:, x2] = sin
    sb[:, x1] = -sin
    return jnp.asarray(ct), jnp.asarray(sa), jnp.asarray(sb)


def _head_slabs(w, width, offset):
    L, K, _ = w.shape
    w = w.reshape(L, K, MLA_HEADS, width)
    w = jnp.pad(w, ((0, 0), (0, 0), (0, 0), (offset, HEAD_PAD - width - offset)))
    return w.reshape(L, K, MLA_HEADS * HEAD_PAD)


def _prep_weights(w_in, w_uq, w_uk, w_uv):
    pool, cq, ckv, kr, qx = (w_in[..., a:b] for a, b in
                             ((0, 512), (512, 896), (896, 1152), (1152, 1184), (1184, 1696)))
    kr_slab = jnp.pad(kr, ((0, 0), (0, 0), (QK_NOPE, LANES - QK_NOPE - QK_ROPE)))
    w_in_p = jnp.concatenate([pool, cq, ckv, qx, kr_slab], axis=-1).astype(BF)
    w_uq_p = _head_slabs(w_uq, QK_NOPE + QK_ROPE, 0).astype(BF)
    w_uk_p = _head_slabs(w_uk, QK_NOPE, 0).astype(BF)
    v_even = _head_slabs(w_uv, V_HEAD, 0).reshape(DEPTH, KV_LORA, MLA_HEADS, HEAD_PAD)
    v_odd = _head_slabs(w_uv, V_HEAD, V_HEAD).reshape(DEPTH, KV_LORA, MLA_HEADS, HEAD_PAD)
    odd = (jnp.arange(MLA_HEADS) % 2 == 1)[None, None, :, None]
    w_uv_p = jnp.where(odd, v_odd, v_even).reshape(DEPTH, KV_LORA, MLA_HEADS * HEAD_PAD).astype(BF)
    vone = np.zeros((MLA_HEADS, HEAD_PAD), np.float32)
    vone[0::2, V_HEAD] = 1.0
    vone[1::2, 0] = 1.0
    return w_in_p, w_uq_p, w_uk_p, w_uv_p, jnp.asarray(vone.reshape(1, -1))


def kernel(x_prompt, x_sample, mem_prompt, mem_sample, w_in, q_norm, kv_norm, w_uq, w_uk, w_uv, pool_mix,
           pool_scale, mem_norm, w_mem_kv, w_branch, w_gate, b_gate, w_out, ln_mix_pre, ln_mix_post,
           ln_ffn_pre, ln_ffn_post, w_gu, w_down):
    w_in_p, w_uq_p, w_uk_p, w_uv_p, vone = _prep_weights(w_in, w_uq, w_uk, w_uv)
    w_mem_kv_b = w_mem_kv.astype(BF)
    w_gate_b, w_branch_b, w_out_b = w_gate.astype(BF), w_branch.astype(BF), w_out.astype(BF)
    pool_mix_b, w_gu_b, w_down_b = pool_mix.astype(BF), w_gu.astype(BF), w_down.astype(BF)
    vec = lambda a: a[:, None, :]
    ln_mix_pre, ln_mix_post, ln_ffn_pre, ln_ffn_post = map(vec, (ln_mix_pre, ln_mix_post, ln_ffn_pre, ln_ffn_post))
    q_norm, kv_norm, b_gate, pool_scale = map(vec, (q_norm, kv_norm, b_gate, pool_scale))

    def trunk(x, mem):
        tabs = _rope_tables(x.shape[1])
        kmem, vmem = _memkv(mem, mem_norm, w_mem_kv_b)
        for l in range(DEPTH):
            u, q, k, v, m_out = _pre(x, tabs, kmem, vmem, l, ln_mix_pre, w_in_p, q_norm, kv_norm,
                                     w_uq_p, w_uk_p, w_uv_p, vone)
            b_out = _attn(q, k, v)
            x = _merge(x, u, b_out, m_out, l, ln_mix_pre, w_gate_b, b_gate, pool_mix_b, pool_scale,
                       w_branch_b, w_out_b, ln_mix_post)
            x = _ffn(x, l, ln_ffn_pre, w_gu_b, w_down_b, ln_ffn_post)
        return x

    return (trunk(x_prompt, mem_prompt), trunk(x_sample, mem_sample))
```

```python
import functools

import jax
import jax.numpy as jnp
import numpy as np
from jax import lax
from jax.experimental import pallas as pl
from jax.experimental.pallas import tpu as pltpu

D_MODEL = 1024
DEPTH = 4
N_MEM = 256
POOL_WINDOWS = (2, 4, 8, 16)
POOL_GROUPS = len(POOL_WINDOWS)
POOL_WIDTH = 512
POOL_GROUP_W = POOL_WIDTH // POOL_GROUPS
MLA_HEADS = 8
QK_NOPE = 64
QK_ROPE = 32
ROPE_HALF = QK_ROPE // 2
V_HEAD = 64
Q_LORA = 384
KV_LORA = 256
MLA_WIDTH = MLA_HEADS * V_HEAD
ROPE_THETA = 10000.0
X_HEADS = 4
X_HEAD_DIM = 128
X_WIDTH = X_HEADS * X_HEAD_DIM
N_BRANCH = 3
BRANCH_WIDTH = 512
D_FF = 2816
EPS = 1e-6

LANES = 128
HEAD_PAD = LANES
HALO = 16
Z_POOL = 0
Z_CQ = Z_POOL + POOL_WIDTH
Z_CKV = Z_CQ + Q_LORA
Z_QX = Z_CKV + KV_LORA
Z_KR = Z_QX + X_WIDTH
Z_WIDTH = Z_KR + LANES
FF_CHUNK = 256
LOG2E = 1.4426950408889634
VMEM_LIMIT = 56 * 1024 * 1024

TM_PRE = 1024
TM_MERGE = 1024
TM_FFN = 1024
ATT_CHUNKS = 4
ATT_MIN_TK = 1024
ATT_TILE_ELEMS = 512 * 2048
ATT_MAX_TQ = 1024
SOFTMAX_ROWS = 32

BF = jnp.bfloat16
F32 = jnp.float32


def _rms(x, g):
    y = x * lax.rsqrt(jnp.mean(x * x, axis=-1, keepdims=True) + EPS)
    return y * g


def _const_spec(shape, layer=None):
    nd = len(shape)
    if layer is None:
        return pl.BlockSpec(shape, lambda *_: (0,) * nd, pipeline_mode=pl.Buffered(1))
    return pl.BlockSpec((None,) + tuple(shape), lambda *_: (layer,) + (0,) * nd, pipeline_mode=pl.Buffered(1))


def _params():
    return pltpu.CompilerParams(vmem_limit_bytes=VMEM_LIMIT)


def _memkv_kernel(mem_ref, g_ref, w_ref, k_ref, v_ref):
    for b in range(mem_ref.shape[0]):
        mn = _rms(mem_ref[b], g_ref[0]).astype(BF)
        kv = jnp.dot(mn, w_ref[0], preferred_element_type=F32)
        k_ref[0, b] = kv[:, :X_WIDTH].astype(BF)
        v_ref[0, b] = kv[:, X_WIDTH:].astype(BF)


def _memkv(mem, mem_norm, w_mem_kv):
    B = mem.shape[0]
    out = jax.ShapeDtypeStruct((DEPTH, B, N_MEM, X_WIDTH), BF)
    return pl.pallas_call(
        _memkv_kernel,
        grid=(DEPTH,),
        in_specs=[
            _const_spec((B, N_MEM, D_MODEL)),
            pl.BlockSpec((1, 1, D_MODEL), lambda l: (l, 0, 0)),
            pl.BlockSpec((1, D_MODEL, 2 * X_WIDTH), lambda l: (l, 0, 0)),
        ],
        out_specs=[
            pl.BlockSpec((1, B, N_MEM, X_WIDTH), lambda l: (l, 0, 0, 0)),
            pl.BlockSpec((1, B, N_MEM, X_WIDTH), lambda l: (l, 0, 0, 0)),
        ],
        out_shape=[out, out],
        name="memkv",
        compiler_params=_params(),
    )(mem, mem_norm.reshape(DEPTH, 1, D_MODEL), w_mem_kv)


def _rope_slab(x, ct, sa, sb):
    return x * ct + pltpu.roll(x, ROPE_HALF, 1) * sa + pltpu.roll(x, LANES - ROPE_HALF, 1) * sb


def _pre_kernel(x_ref, ct_ref, sa_ref, sb_ref, qct_ref, qsa_ref, qsb_ref, kmem_ref, vmem_ref, ln_ref, w_in_ref,
                qn_ref, kvn_ref, w_uq_ref, w_uk_ref, w_uv_ref, vone_ref,
                u_ref, q_ref, k_ref, v_ref, m_ref):
    h = _rms(x_ref[0], ln_ref[...]).astype(BF)
    z = jnp.dot(h, w_in_ref[...], preferred_element_type=F32)
    u_ref[0] = z[:, Z_POOL:Z_POOL + POOL_WIDTH]

    ct, sa, sb = ct_ref[...], sa_ref[...], sb_ref[...]
    qct, qsa, qsb = qct_ref[...], qsa_ref[...], qsb_ref[...]
    cq = _rms(z[:, Z_CQ:Z_CQ + Q_LORA], qn_ref[...]).astype(BF)
    qf = jnp.dot(cq, w_uq_ref[...], preferred_element_type=F32)
    c = _rms(z[:, Z_CKV:Z_CKV + KV_LORA], kvn_ref[...]).astype(BF)
    kf = jnp.dot(c, w_uk_ref[...], preferred_element_type=F32)
    vf = jnp.dot(c, w_uv_ref[...], preferred_element_type=F32) + vone_ref[...]
    kr = _rope_slab(z[:, Z_KR:Z_KR + LANES], ct, sa, sb)
    for hd in range(MLA_HEADS):
        sl = slice(hd * HEAD_PAD, (hd + 1) * HEAD_PAD)
        q_ref[0, hd] = _rope_slab(qf[:, sl], qct, qsa, qsb).astype(BF)
        k_ref[0, hd] = (kf[:, sl] + kr).astype(BF)
        v_ref[0, hd] = vf[:, sl].astype(BF)

    x_scale = X_HEAD_DIM ** -0.5 * LOG2E
    qx = (z[:, Z_QX:Z_QX + X_WIDTH] * x_scale).astype(BF)
    outs = []
    for hd in range(X_HEADS):
        sl = slice(hd * X_HEAD_DIM, (hd + 1) * X_HEAD_DIM)
        s = lax.dot_general(qx[:, sl], kmem_ref[0, 0, :, sl], (((1,), (1,)), ((), ())),
                            preferred_element_type=F32)
        p = jnp.exp2(s - jnp.max(s, axis=-1, keepdims=True))
        l = jnp.sum(p, axis=-1, keepdims=True)
        o = jnp.dot(p.astype(BF), vmem_ref[0, 0, :, sl], preferred_element_type=F32)
        outs.append(o / l)
    m_ref[0] = jnp.concatenate(outs, axis=-1).astype(BF)


def _pre(x, tabs, kmem, vmem, l, ln, w_in, qn, kvn, w_uq, w_uk, w_uv, vone):
    B, S, _ = x.shape
    tm = TM_PRE
    row = lambda b, i: (b, i, 0)
    head = lambda b, i: (b, 0, i, 0)
    tab = lambda b, i: (i, 0)
    att = jax.ShapeDtypeStruct((B, MLA_HEADS, S, HEAD_PAD), BF)
    return pl.pallas_call(
        _pre_kernel,
        grid=(B, S // tm),
        in_specs=[
            pl.BlockSpec((1, tm, D_MODEL), row),
            pl.BlockSpec((tm, LANES), tab), pl.BlockSpec((tm, LANES), tab), pl.BlockSpec((tm, LANES), tab),
            pl.BlockSpec((tm, LANES), tab), pl.BlockSpec((tm, LANES), tab), pl.BlockSpec((tm, LANES), tab),
            pl.BlockSpec((1, 1, N_MEM, X_WIDTH), lambda b, i: (l, b, 0, 0)),
            pl.BlockSpec((1, 1, N_MEM, X_WIDTH), lambda b, i: (l, b, 0, 0)),
            _const_spec((1, D_MODEL), l), _const_spec((D_MODEL, Z_WIDTH), l),
            _const_spec((1, Q_LORA), l), _const_spec((1, KV_LORA), l),
            _const_spec((Q_LORA, MLA_HEADS * HEAD_PAD), l),
            _const_spec((KV_LORA, MLA_HEADS * HEAD_PAD), l),
            _const_spec((KV_LORA, MLA_HEADS * HEAD_PAD), l),
            _const_spec((1, MLA_HEADS * HEAD_PAD)),
        ],
        out_specs=[
            pl.BlockSpec((1, tm, POOL_WIDTH), row),
            pl.BlockSpec((1, MLA_HEADS, tm, HEAD_PAD), head),
            pl.BlockSpec((1, MLA_HEADS, tm, HEAD_PAD), head),
            pl.BlockSpec((1, MLA_HEADS, tm, HEAD_PAD), head),
            pl.BlockSpec((1, tm, X_WIDTH), row),
        ],
        out_shape=[
            jax.ShapeDtypeStruct((B, S, POOL_WIDTH), F32), att, att, att,
            jax.ShapeDtypeStruct((B, S, X_WIDTH), BF),
        ],
        name="pre",
        compiler_params=_params(),
    )(x, *tabs, kmem, vmem, ln, w_in, qn, kvn, w_uq, w_uk, w_uv, vone)


def _attn_kernel(q_ref, k_ref, v_ref, o_ref, s_scr, p_scr, m_scr, a_scr, acc_scr, *, n_chunks, tk, tq):
    heads = range(q_ref.shape[1])
    nt = (((1,), (1,)), ((), ()))
    lax.fori_loop(0, q_ref.shape[2] // tq,
                  functools.partial(_attn_tile, q_ref, k_ref, v_ref, o_ref, s_scr, p_scr, m_scr, a_scr, acc_scr,
                                    heads, nt, n_chunks, tk, tq), 0)


def _attn_tile(q_ref, k_ref, v_ref, o_ref, s_scr, p_scr, m_scr, a_scr, acc_scr, heads, nt, n_chunks, tk, tq, t, carry):
    row0 = pl.multiple_of(t * tq, tq)

    def scores(j, c):
        s_scr[j, c % 2] = lax.dot_general(q_ref[0, j, pl.ds(row0, tq), :], k_ref[0, j, c * tk:(c + 1) * tk, :], nt,
                                          preferred_element_type=F32)

    def softmax(j, c):
        slot = c % 2
        for r in range(tq // SOFTMAX_ROWS):
            rows = slice(r * SOFTMAX_ROWS, (r + 1) * SOFTMAX_ROWS)
            mx = s_scr[j, slot, rows, 0:LANES]
            for kb in range(1, tk // LANES):
                mx = jnp.maximum(mx, s_scr[j, slot, rows, kb * LANES:(kb + 1) * LANES])
            mx = jnp.broadcast_to(jnp.max(mx, axis=-1, keepdims=True), (SOFTMAX_ROWS, LANES))
            if c == 0:
                m_new = mx
            else:
                m_old = m_scr[j, rows, :]
                m_new = jnp.maximum(m_old, mx)
                a_scr[j, slot, rows, :] = jnp.exp2(m_old - m_new)
            for kb in range(tk // LANES):
                cols = slice(kb * LANES, (kb + 1) * LANES)
                p_scr[j, slot, rows, cols] = jnp.exp2(s_scr[j, slot, rows, cols] - m_new).astype(BF)
            m_scr[j, rows, :] = m_new

    def values(j, c):
        slot = c % 2
        pv = jnp.dot(p_scr[j, slot], v_ref[0, j, c * tk:(c + 1) * tk, :], preferred_element_type=F32)
        acc_scr[j] = pv if c == 0 else acc_scr[j] * a_scr[j, slot] + pv

    for j in heads:
        scores(j, 0)
    for c in range(n_chunks):
        for j in heads:
            if c + 1 < n_chunks:
                scores(j, c + 1)
            softmax(j, c)
            if c >= 1:
                values(j, c - 1)
    for j in heads:
        values(j, n_chunks - 1)
    o0 = acc_scr[0] / acc_scr[0][:, V_HEAD:V_HEAD + 1]
    o1 = acc_scr[1] / acc_scr[1][:, 0:1]
    lane = lax.broadcasted_iota(jnp.int32, (tq, HEAD_PAD), 1)
    o_ref[0, pl.ds(row0, tq), :] = jnp.where(lane < V_HEAD, o0, o1).astype(BF)
    return carry


def _attn(q, k, v):
    B, H, S, _ = q.shape
    tk = max(S // ATT_CHUNKS, ATT_MIN_TK)
    tq = min(ATT_TILE_ELEMS // tk, ATT_MAX_TQ)
    assert S % tk == 0 and tk % LANES == 0 and S % tq == 0 and tq % SOFTMAX_ROWS == 0
    kern = functools.partial(_attn_kernel, n_chunks=S // tk, tk=tk, tq=tq)
    whole = lambda b, hp: (b, hp, 0, 0)
    return pl.pallas_call(
        kern,
        grid=(B, H // 2),
        in_specs=[
            pl.BlockSpec((1, 2, S, HEAD_PAD), whole),
            pl.BlockSpec((1, 2, S, HEAD_PAD), whole),
            pl.BlockSpec((1, 2, S, HEAD_PAD), whole),
        ],
        out_specs=pl.BlockSpec((1, S, 2 * V_HEAD), lambda b, hp: (b, 0, hp)),
        out_shape=jax.ShapeDtypeStruct((B, S, MLA_WIDTH), BF),
        scratch_shapes=[
            pltpu.VMEM((2, 2, tq, tk), F32),
            pltpu.VMEM((2, 2, tq, tk), BF),
            pltpu.VMEM((2, tq, LANES), F32),
            pltpu.VMEM((2, 2, tq, LANES), F32),
            pltpu.VMEM((2, tq, HEAD_PAD), F32),
        ],
        name="attn",
        compiler_params=_params(),
    )(q, k, v)


def _window_sum(ubuf, pbuf, lanes, w, tm):
    half = w // 2
    src, rows, m, slot = (lambda a, b: ubuf[a:b, lanes]), tm + 3 * HALO, 1, 0
    while m < half:
        rows -= 8
        pbuf[slot, 0:rows, :] = src(0, rows) + src(m, rows + m)
        src = functools.partial(lambda s, a, b: pbuf[s, a:b, :], slot)
        m, slot = 2 * m, 1 - slot
    return src(HALO - half, HALO - half + tm) + src(HALO, HALO + tm)


def _merge_kernel(x_ref, u_ref, up_ref, un_ref, icnt_ref, b_ref, m_ref, ln_ref, wg_ref, bg_ref, mix_ref, ps_ref,
                  wbr_ref, wo_ref, lnp_ref, o_ref, ubuf, pbuf):
    tm = x_ref.shape[1]
    i = pl.program_id(1)
    x = x_ref[0]
    h = _rms(x, ln_ref[...]).astype(BF)

    def gate(n):
        sl = slice(n * D_MODEL, (n + 1) * D_MODEL)
        return jax.nn.sigmoid(jnp.dot(h, wg_ref[:, sl], preferred_element_type=F32) + bg_ref[:, sl])

    merged = (gate(1) * jnp.dot(b_ref[0], wbr_ref[1], preferred_element_type=F32)
              + gate(2) * jnp.dot(m_ref[0], wbr_ref[2], preferred_element_type=F32))
    gate_pool = gate(0)

    ubuf[0:HALO, :] = jnp.where(i > 0, up_ref[0], 0.0)
    ubuf[HALO:HALO + tm, :] = u_ref[0]
    ubuf[HALO + tm:2 * HALO + tm, :] = jnp.where(i < pl.num_programs(1) - 1, un_ref[0], 0.0)
    ubuf[2 * HALO + tm:, :] = jnp.zeros((HALO, POOL_WIDTH), F32)
    icnt = icnt_ref[...]
    a_parts = []
    for g, w in enumerate(POOL_WINDOWS):
        sl = slice(g * POOL_GROUP_W, (g + 1) * POOL_GROUP_W)
        tot = _window_sum(ubuf, pbuf, sl, w, tm)
        diff = (tot * icnt[:, g:g + 1] - ubuf[HALO:HALO + tm, sl]).astype(BF)
        a_parts.append(jnp.dot(diff, mix_ref[g], preferred_element_type=F32))
    a_out = (jnp.concatenate(a_parts, axis=-1) * ps_ref[...]).astype(BF)

    merged = merged + gate_pool * jnp.dot(a_out, wbr_ref[0], preferred_element_type=F32)
    merged = merged.astype(BF)
    half = tm // 2
    for r in range(2):
        rows = slice(r * half, (r + 1) * half)
        y = jnp.dot(merged[rows, :], wo_ref[...], preferred_element_type=F32)
        o_ref[0, rows, :] = x[rows, :] + _rms(y, lnp_ref[...])


def _merge(x, u, b_out, m_out, l, ln, wg, bg, mix, ps, wbr, wo, lnp):
    B, S, _ = x.shape
    tm = TM_MERGE
    hb = tm // HALO
    n_halo = S // HALO
    row = lambda b, i: (b, i, 0)
    t = np.arange(S)
    icnt = np.stack([1.0 / (np.minimum(t + w // 2, S) - np.maximum(t - w // 2, 0)) for w in POOL_WINDOWS], axis=1)
    return pl.pallas_call(
        _merge_kernel,
        grid=(B, S // tm),
        in_specs=[
            pl.BlockSpec((1, tm, D_MODEL), row),
            pl.BlockSpec((1, tm, POOL_WIDTH), row),
            pl.BlockSpec((1, HALO, POOL_WIDTH), lambda b, i: (b, jnp.maximum(i * hb - 1, 0), 0)),
            pl.BlockSpec((1, HALO, POOL_WIDTH), lambda b, i: (b, jnp.minimum((i + 1) * hb, n_halo - 1), 0)),
            pl.BlockSpec((tm, POOL_GROUPS), lambda b, i: (i, 0)),
            pl.BlockSpec((1, tm, MLA_WIDTH), row),
            pl.BlockSpec((1, tm, X_WIDTH), row),
            _const_spec((1, D_MODEL), l),
            _const_spec((D_MODEL, N_BRANCH * D_MODEL), l),
            _const_spec((1, N_BRANCH * D_MODEL), l),
            _const_spec((POOL_GROUPS, POOL_GROUP_W, POOL_GROUP_W), l),
            _const_spec((1, POOL_WIDTH), l),
            _const_spec((N_BRANCH, BRANCH_WIDTH, D_MODEL), l),
            _const_spec((D_MODEL, D_MODEL), l),
            _const_spec((1, D_MODEL), l),
        ],
        out_specs=pl.BlockSpec((1, tm, D_MODEL), row),
        out_shape=jax.ShapeDtypeStruct((B, S, D_MODEL), F32),
        scratch_shapes=[
            pltpu.VMEM((tm + 3 * HALO, POOL_WIDTH), F32),
            pltpu.VMEM((2, tm + 3 * HALO - 8, POOL_GROUP_W), F32),
        ],
        name="merge",
        compiler_params=_params(),
    )(x, u, u, u, jnp.asarray(icnt, F32), b_out, m_out, ln, wg, bg, mix, ps, wbr, wo, lnp)


def _ffn_kernel(x_ref, ln_ref, wgu_ref, wd_ref, lnp_ref, o_ref):
    half = x_ref.shape[0] // 2
    for r in range(2):
        rows = slice(r * half, (r + 1) * half)
        x = x_ref[rows, :]
        h = _rms(x, ln_ref[...]).astype(BF)
        f = None
        for c in range(D_FF // FF_CHUNK):
            g = jnp.dot(h, wgu_ref[:, c * FF_CHUNK:(c + 1) * FF_CHUNK], preferred_element_type=F32)
            u = jnp.dot(h, wgu_ref[:, D_FF + c * FF_CHUNK:D_FF + (c + 1) * FF_CHUNK], preferred_element_type=F32)
            a = (g * jax.nn.sigmoid(g) * u).astype(BF)
            part = jnp.dot(a, wd_ref[c * FF_CHUNK:(c + 1) * FF_CHUNK, :], preferred_element_type=F32)
            f = part if f is None else f + part
        o_ref[rows, :] = x + _rms(f, lnp_ref[...])


def _ffn(x, l, ln, wgu, wd, lnp):
    B, S, _ = x.shape
    n = B * S
    tm = TM_FFN
    out = pl.pallas_call(
        _ffn_kernel,
        grid=(n // tm,),
        in_specs=[
            pl.BlockSpec((tm, D_MODEL), lambda i: (i, 0)),
            _const_spec((1, D_MODEL), l),
            _const_spec((D_MODEL, 2 * D_FF), l),
            _const_spec((D_FF, D_MODEL), l),
            _const_spec((1, D_MODEL), l),
        ],
        out_specs=pl.BlockSpec((tm, D_MODEL), lambda i: (i, 0)),
        out_shape=jax.ShapeDtypeStruct((n, D_MODEL), F32),
        name="ffn",
        compiler_params=_params(),
    )(x.reshape(n, D_MODEL), ln, wgu, wd, lnp)
    return out.reshape(B, S, D_MODEL)


def _rope_tables(seq):
    inv = 1.0 / (ROPE_THETA ** (np.arange(0, QK_ROPE, 2, dtype=np.float32) / QK_ROPE))
    ang = np.arange(seq, dtype=np.float32)[:, None] * inv[None, :]
    cos, sin = np.cos(ang), np.sin(ang)
    ct = np.zeros((seq, LANES), np.float32)
    sa = np.zeros((seq, LANES), np.float32)
    sb = np.zeros((seq, LANES), np.float32)
    x1 = slice(QK_NOPE, QK_NOPE + ROPE_HALF)
    x2 = slice(QK_NOPE + ROPE_HALF, QK_NOPE + QK_ROPE)
    ct[:, :QK_NOPE] = 1.0
    ct[:, x1] = cos
    ct[:, x2] = cos
    sa[:, x2] = sin
    sb[:, x1] = -sin
    q_scale = np.float32((QK_NOPE + QK_ROPE) ** -0.5 * LOG2E)
    return tuple(jnp.asarray(t) for t in (ct, sa, sb, ct * q_scale, sa * q_scale, sb * q_scale))


def _head_slabs(w, width, offset):
    L, K, _ = w.shape
    w = w.reshape(L, K, MLA_HEADS, width)
    w = jnp.pad(w, ((0, 0), (0, 0), (0, 0), (offset, HEAD_PAD - width - offset)))
    return w.reshape(L, K, MLA_HEADS * HEAD_PAD)


def _prep_weights(w_in, w_uq, w_uk, w_uv):
    pool, cq, ckv, kr, qx = (w_in[..., a:b] for a, b in
                             ((0, 512), (512, 896), (896, 1152), (1152, 1184), (1184, 1696)))
    kr_slab = jnp.pad(kr, ((0, 0), (0, 0), (QK_NOPE, LANES - QK_NOPE - QK_ROPE)))
    w_in_p = jnp.concatenate([pool, cq, ckv, qx, kr_slab], axis=-1).astype(BF)
    w_uq_p = _head_slabs(w_uq, QK_NOPE + QK_ROPE, 0).astype(BF)
    w_uk_p = _head_slabs(w_uk, QK_NOPE, 0).astype(BF)
    v_even = _head_slabs(w_uv, V_HEAD, 0).reshape(DEPTH, KV_LORA, MLA_HEADS, HEAD_PAD)
    v_odd = _head_slabs(w_uv, V_HEAD, V_HEAD).reshape(DEPTH, KV_LORA, MLA_HEADS, HEAD_PAD)
    odd = (jnp.arange(MLA_HEADS) % 2 == 1)[None, None, :, None]
    w_uv_p = jnp.where(odd, v_odd, v_even).reshape(DEPTH, KV_LORA, MLA_HEADS * HEAD_PAD).astype(BF)
    vone = np.zeros((MLA_HEADS, HEAD_PAD), np.float32)
    vone[0::2, V_HEAD] = 1.0
    vone[1::2, 0] = 1.0
    return w_in_p, w_uq_p, w_uk_p, w_uv_p, jnp.asarray(vone.reshape(1, -1))


def kernel(x_prompt, x_sample, mem_prompt, mem_sample, w_in, q_norm, kv_norm, w_uq, w_uk, w_uv, pool_mix,
           pool_scale, mem_norm, w_mem_kv, w_branch, w_gate, b_gate, w_out, ln_mix_pre, ln_mix_post,
           ln_ffn_pre, ln_ffn_post, w_gu, w_down):
    w_in_p, w_uq_p, w_uk_p, w_uv_p, vone = _prep_weights(w_in, w_uq, w_uk, w_uv)
    w_mem_kv_b = w_mem_kv.astype(BF)
    w_gate_b, w_branch_b, w_out_b = w_gate.astype(BF), w_branch.astype(BF), w_out.astype(BF)
    pool_mix_b, w_gu_b, w_down_b = pool_mix.astype(BF), w_gu.astype(BF), w_down.astype(BF)
    vec = lambda a: a[:, None, :]
    ln_mix_pre, ln_mix_post, ln_ffn_pre, ln_ffn_post = map(vec, (ln_mix_pre, ln_mix_post, ln_ffn_pre, ln_ffn_post))
    q_norm, kv_norm, b_gate, pool_scale = map(vec, (q_norm, kv_norm, b_gate, pool_scale))

    def trunk(x, mem):
        tabs = _rope_tables(x.shape[1])
        kmem, vmem = _memkv(mem, mem_norm, w_mem_kv_b)
        for l in range(DEPTH):
            u, q, k, v, m_out = _pre(x, tabs, kmem, vmem, l, ln_mix_pre, w_in_p, q_norm, kv_norm,
                                     w_uq_p, w_uk_p, w_uv_p, vone)
            b_out = _attn(q, k, v)
            x = _merge(x, u, b_out, m_out, l, ln_mix_pre, w_gate_b, b_gate, pool_mix_b, pool_scale,
                       w_branch_b, w_out_b, ln_mix_post)
            x = _ffn(x, l, ln_ffn_pre, w_gu_b, w_down_b, ln_ffn_post)
        return x

    return (trunk(x_prompt, mem_prompt), trunk(x_sample, mem_sample))
```

```python
import functools

import jax
import jax.numpy as jnp
import numpy as np
from jax import lax
from jax.experimental import pallas as pl
from jax.experimental.pallas import tpu as pltpu

D_MODEL = 1024
DEPTH = 4
N_MEM = 256
POOL_WINDOWS = (2, 4, 8, 16)
POOL_GROUPS = len(POOL_WINDOWS)
POOL_WIDTH = 512
POOL_GROUP_W = POOL_WIDTH // POOL_GROUPS
MLA_HEADS = 8
QK_NOPE = 64
QK_ROPE = 32
ROPE_HALF = QK_ROPE // 2
V_HEAD = 64
Q_LORA = 384
KV_LORA = 256
MLA_WIDTH = MLA_HEADS * V_HEAD
ROPE_THETA = 10000.0
X_HEADS = 4
X_HEAD_DIM = 128
X_WIDTH = X_HEADS * X_HEAD_DIM
N_BRANCH = 3
BRANCH_WIDTH = 512
D_FF = 2816
EPS = 1e-6

LANES = 128
HEAD_PAD = LANES
HALO = 16
Z_POOL = 0
Z_CQ = Z_POOL + POOL_WIDTH
Z_CKV = Z_CQ + Q_LORA
Z_QX = Z_CKV + KV_LORA
Z_KR = Z_QX + X_WIDTH
Z_WIDTH = Z_KR + LANES
FF_CHUNK = 256
LOG2E = 1.4426950408889634
VMEM_LIMIT = 56 * 1024 * 1024

TM_PRE = 1024
TM_MERGE = 1024
TM_FFN = 1024
ATT_CHUNKS = 4
ATT_MIN_TK = 1024
ATT_TILE_ELEMS = 512 * 2048
ATT_MAX_TQ = 1024
SOFTMAX_ROWS = 32

BF = jnp.bfloat16
F32 = jnp.float32


def _rms(x, g):
    y = x * lax.rsqrt(jnp.mean(x * x, axis=-1, keepdims=True) + EPS)
    return y * g


def _const_spec(shape, layer=None):
    nd = len(shape)
    if layer is None:
        return pl.BlockSpec(shape, lambda *_: (0,) * nd, pipeline_mode=pl.Buffered(1))
    return pl.BlockSpec((None,) + tuple(shape), lambda *_: (layer,) + (0,) * nd, pipeline_mode=pl.Buffered(1))


def _params():
    return pltpu.CompilerParams(vmem_limit_bytes=VMEM_LIMIT)


def _memkv_kernel(mem_ref, g_ref, w_ref, k_ref, v_ref):
    for b in range(mem_ref.shape[0]):
        mn = _rms(mem_ref[b], g_ref[0]).astype(BF)
        kv = jnp.dot(mn, w_ref[0], preferred_element_type=F32)
        k_ref[0, b] = kv[:, :X_WIDTH].astype(BF)
        v_ref[0, b] = kv[:, X_WIDTH:].astype(BF)


def _memkv(mem, mem_norm, w_mem_kv):
    B = mem.shape[0]
    out = jax.ShapeDtypeStruct((DEPTH, B, N_MEM, X_WIDTH), BF)
    return pl.pallas_call(
        _memkv_kernel,
        grid=(DEPTH,),
        in_specs=[
            _const_spec((B, N_MEM, D_MODEL)),
            pl.BlockSpec((1, 1, D_MODEL), lambda l: (l, 0, 0)),
            pl.BlockSpec((1, D_MODEL, 2 * X_WIDTH), lambda l: (l, 0, 0)),
        ],
        out_specs=[
            pl.BlockSpec((1, B, N_MEM, X_WIDTH), lambda l: (l, 0, 0, 0)),
            pl.BlockSpec((1, B, N_MEM, X_WIDTH), lambda l: (l, 0, 0, 0)),
        ],
        out_shape=[out, out],
        name="memkv",
        compiler_params=_params(),
    )(mem, mem_norm.reshape(DEPTH, 1, D_MODEL), w_mem_kv)


def _rope_slab(x, ct, sa, sb):
    return x * ct + pltpu.roll(x, ROPE_HALF, 1) * sa + pltpu.roll(x, LANES - ROPE_HALF, 1) * sb


def _pre_kernel(x_ref, ct_ref, sa_ref, sb_ref, kmem_ref, vmem_ref, ln_ref, w_in_ref, qn_ref, kvn_ref,
                w_uq_ref, w_uk_ref, w_uv_ref, vone_ref,
                u_ref, q_ref, k_ref, v_ref, m_ref):
    h = _rms(x_ref[0], ln_ref[...]).astype(BF)
    z = jnp.dot(h, w_in_ref[...], preferred_element_type=F32)
    u_ref[0] = z[:, Z_POOL:Z_POOL + POOL_WIDTH]

    ct, sa, sb = ct_ref[...], sa_ref[...], sb_ref[...]
    q_scale = (QK_NOPE + QK_ROPE) ** -0.5 * LOG2E
    cq = _rms(z[:, Z_CQ:Z_CQ + Q_LORA], qn_ref[...]).astype(BF)
    qf = jnp.dot(cq, w_uq_ref[...], preferred_element_type=F32)
    c = _rms(z[:, Z_CKV:Z_CKV + KV_LORA], kvn_ref[...]).astype(BF)
    kf = jnp.dot(c, w_uk_ref[...], preferred_element_type=F32)
    vf = jnp.dot(c, w_uv_ref[...], preferred_element_type=F32) + vone_ref[...]
    kr = _rope_slab(z[:, Z_KR:Z_KR + LANES], ct, sa, sb)
    for hd in range(MLA_HEADS):
        sl = slice(hd * HEAD_PAD, (hd + 1) * HEAD_PAD)
        q_ref[0, hd] = (_rope_slab(qf[:, sl], ct, sa, sb) * q_scale).astype(BF)
        k_ref[0, hd] = (kf[:, sl] + kr).astype(BF)
        v_ref[0, hd] = vf[:, sl].astype(BF)

    x_scale = X_HEAD_DIM ** -0.5 * LOG2E
    qx = (z[:, Z_QX:Z_QX + X_WIDTH] * x_scale).astype(BF)
    outs = []
    for hd in range(X_HEADS):
        sl = slice(hd * X_HEAD_DIM, (hd + 1) * X_HEAD_DIM)
        s = lax.dot_general(qx[:, sl], kmem_ref[0, 0, :, sl], (((1,), (1,)), ((), ())),
                            preferred_element_type=F32)
        p = jnp.exp2(s - jnp.max(s, axis=-1, keepdims=True))
        l = jnp.sum(p, axis=-1, keepdims=True)
        o = jnp.dot(p.astype(BF), vmem_ref[0, 0, :, sl], preferred_element_type=F32)
        outs.append(o / l)
    m_ref[0] = jnp.concatenate(outs, axis=-1).astype(BF)


def _pre(x, tabs, kmem, vmem, l, ln, w_in, qn, kvn, w_uq, w_uk, w_uv, vone):
    B, S, _ = x.shape
    tm = TM_PRE
    row = lambda b, i: (b, i, 0)
    head = lambda b, i: (b, 0, i, 0)
    tab = lambda b, i: (i, 0)
    att = jax.ShapeDtypeStruct((B, MLA_HEADS, S, HEAD_PAD), BF)
    return pl.pallas_call(
        _pre_kernel,
        grid=(B, S // tm),
        in_specs=[
            pl.BlockSpec((1, tm, D_MODEL), row),
            pl.BlockSpec((tm, LANES), tab), pl.BlockSpec((tm, LANES), tab), pl.BlockSpec((tm, LANES), tab),
            pl.BlockSpec((1, 1, N_MEM, X_WIDTH), lambda b, i: (l, b, 0, 0)),
            pl.BlockSpec((1, 1, N_MEM, X_WIDTH), lambda b, i: (l, b, 0, 0)),
            _const_spec((1, D_MODEL), l), _const_spec((D_MODEL, Z_WIDTH), l),
            _const_spec((1, Q_LORA), l), _const_spec((1, KV_LORA), l),
            _const_spec((Q_LORA, MLA_HEADS * HEAD_PAD), l),
            _const_spec((KV_LORA, MLA_HEADS * HEAD_PAD), l),
            _const_spec((KV_LORA, MLA_HEADS * HEAD_PAD), l),
            _const_spec((1, MLA_HEADS * HEAD_PAD)),
        ],
        out_specs=[
            pl.BlockSpec((1, tm, POOL_WIDTH), row),
            pl.BlockSpec((1, MLA_HEADS, tm, HEAD_PAD), head),
            pl.BlockSpec((1, MLA_HEADS, tm, HEAD_PAD), head),
            pl.BlockSpec((1, MLA_HEADS, tm, HEAD_PAD), head),
            pl.BlockSpec((1, tm, X_WIDTH), row),
        ],
        out_shape=[
            jax.ShapeDtypeStruct((B, S, POOL_WIDTH), F32), att, att, att,
            jax.ShapeDtypeStruct((B, S, X_WIDTH), BF),
        ],
        name="pre",
        compiler_params=_params(),
    )(x, *tabs, kmem, vmem, ln, w_in, qn, kvn, w_uq, w_uk, w_uv, vone)


def _attn_kernel(q_ref, k_ref, v_ref, o_ref, s_scr, p_scr, m_scr, a_scr, acc_scr, *, n_chunks, tk, tq):
    heads = range(q_ref.shape[1])
    nt = (((1,), (1,)), ((), ()))
    lax.fori_loop(0, q_ref.shape[2] // tq,
                  functools.partial(_attn_tile, q_ref, k_ref, v_ref, o_ref, s_scr, p_scr, m_scr, a_scr, acc_scr,
                                    heads, nt, n_chunks, tk, tq), 0)


def _attn_tile(q_ref, k_ref, v_ref, o_ref, s_scr, p_scr, m_scr, a_scr, acc_scr, heads, nt, n_chunks, tk, tq, t, carry):
    row0 = pl.multiple_of(t * tq, tq)

    def scores(j, c):
        s_scr[j, c % 2] = lax.dot_general(q_ref[0, j, pl.ds(row0, tq), :], k_ref[0, j, c * tk:(c + 1) * tk, :], nt,
                                          preferred_element_type=F32)

    def softmax(j, c):
        slot = c % 2
        for r in range(tq // SOFTMAX_ROWS):
            rows = slice(r * SOFTMAX_ROWS, (r + 1) * SOFTMAX_ROWS)
            mx = s_scr[j, slot, rows, 0:LANES]
            for kb in range(1, tk // LANES):
                mx = jnp.maximum(mx, s_scr[j, slot, rows, kb * LANES:(kb + 1) * LANES])
            mx = jnp.broadcast_to(jnp.max(mx, axis=-1, keepdims=True), (SOFTMAX_ROWS, LANES))
            if c == 0:
                m_new = mx
            else:
                m_old = m_scr[j, rows, :]
                m_new = jnp.maximum(m_old, mx)
                a_scr[j, slot, rows, :] = jnp.exp2(m_old - m_new)
            for kb in range(tk // LANES):
                cols = slice(kb * LANES, (kb + 1) * LANES)
                p_scr[j, slot, rows, cols] = jnp.exp2(s_scr[j, slot, rows, cols] - m_new).astype(BF)
            m_scr[j, rows, :] = m_new

    def values(j, c):
        slot = c % 2
        pv = jnp.dot(p_scr[j, slot], v_ref[0, j, c * tk:(c + 1) * tk, :], preferred_element_type=F32)
        acc_scr[j] = pv if c == 0 else acc_scr[j] * a_scr[j, slot] + pv

    for j in heads:
        scores(j, 0)
    for c in range(n_chunks):
        for j in heads:
            if c + 1 < n_chunks:
                scores(j, c + 1)
            softmax(j, c)
            if c >= 1:
                values(j, c - 1)
    for j in heads:
        values(j, n_chunks - 1)
    o0 = acc_scr[0] / acc_scr[0][:, V_HEAD:V_HEAD + 1]
    o1 = acc_scr[1] / acc_scr[1][:, 0:1]
    lane = lax.broadcasted_iota(jnp.int32, (tq, HEAD_PAD), 1)
    o_ref[0, pl.ds(row0, tq), :] = jnp.where(lane < V_HEAD, o0, o1).astype(BF)
    return carry


def _attn(q, k, v):
    B, H, S, _ = q.shape
    tk = max(S // ATT_CHUNKS, ATT_MIN_TK)
    tq = min(ATT_TILE_ELEMS // tk, ATT_MAX_TQ)
    assert S % tk == 0 and tk % LANES == 0 and S % tq == 0 and tq % SOFTMAX_ROWS == 0
    kern = functools.partial(_attn_kernel, n_chunks=S // tk, tk=tk, tq=tq)
    whole = lambda b, hp: (b, hp, 0, 0)
    return pl.pallas_call(
        kern,
        grid=(B, H // 2),
        in_specs=[
            pl.BlockSpec((1, 2, S, HEAD_PAD), whole),
            pl.BlockSpec((1, 2, S, HEAD_PAD), whole),
            pl.BlockSpec((1, 2, S, HEAD_PAD), whole),
        ],
        out_specs=pl.BlockSpec((1, S, 2 * V_HEAD), lambda b, hp: (b, 0, hp)),
        out_shape=jax.ShapeDtypeStruct((B, S, MLA_WIDTH), BF),
        scratch_shapes=[
            pltpu.VMEM((2, 2, tq, tk), F32),
            pltpu.VMEM((2, 2, tq, tk), BF),
            pltpu.VMEM((2, tq, LANES), F32),
            pltpu.VMEM((2, 2, tq, LANES), F32),
            pltpu.VMEM((2, tq, HEAD_PAD), F32),
        ],
        name="attn",
        compiler_params=_params(),
    )(q, k, v)


def _window_sum(ubuf, pbuf, lanes, w, tm):
    half = w // 2
    src, rows, m, slot = (lambda a, b: ubuf[a:b, lanes]), tm + 3 * HALO, 1, 0
    while m < half:
        rows -= 8
        pbuf[slot, 0:rows, :] = src(0, rows) + src(m, rows + m)
        src = functools.partial(lambda s, a, b: pbuf[s, a:b, :], slot)
        m, slot = 2 * m, 1 - slot
    return src(HALO - half, HALO - half + tm) + src(HALO, HALO + tm)


def _merge_kernel(x_ref, u_ref, up_ref, un_ref, icnt_ref, b_ref, m_ref, ln_ref, wg_ref, bg_ref, mix_ref, ps_ref,
                  wbr_ref, wo_ref, lnp_ref, o_ref, ubuf, pbuf):
    tm = x_ref.shape[1]
    i = pl.program_id(1)
    x = x_ref[0]
    h = _rms(x, ln_ref[...]).astype(BF)

    def gate(n):
        sl = slice(n * D_MODEL, (n + 1) * D_MODEL)
        return jax.nn.sigmoid(jnp.dot(h, wg_ref[:, sl], preferred_element_type=F32) + bg_ref[:, sl])

    merged = (gate(1) * jnp.dot(b_ref[0], wbr_ref[1], preferred_element_type=F32)
              + gate(2) * jnp.dot(m_ref[0], wbr_ref[2], preferred_element_type=F32))
    gate_pool = gate(0)

    ubuf[0:HALO, :] = jnp.where(i > 0, up_ref[0], 0.0)
    ubuf[HALO:HALO + tm, :] = u_ref[0]
    ubuf[HALO + tm:2 * HALO + tm, :] = jnp.where(i < pl.num_programs(1) - 1, un_ref[0], 0.0)
    ubuf[2 * HALO + tm:, :] = jnp.zeros((HALO, POOL_WIDTH), F32)
    icnt = icnt_ref[...]
    a_parts = []
    for g, w in enumerate(POOL_WINDOWS):
        sl = slice(g * POOL_GROUP_W, (g + 1) * POOL_GROUP_W)
        tot = _window_sum(ubuf, pbuf, sl, w, tm)
        diff = (tot * icnt[:, g:g + 1] - ubuf[HALO:HALO + tm, sl]).astype(BF)
        a_parts.append(jnp.dot(diff, mix_ref[g], preferred_element_type=F32))
    a_out = (jnp.concatenate(a_parts, axis=-1) * ps_ref[...]).astype(BF)

    merged = merged + gate_pool * jnp.dot(a_out, wbr_ref[0], preferred_element_type=F32)
    merged = merged.astype(BF)
    half = tm // 2
    for r in range(2):
        rows = slice(r * half, (r + 1) * half)
        y = jnp.dot(merged[rows, :], wo_ref[...], preferred_element_type=F32)
        o_ref[0, rows, :] = x[rows, :] + _rms(y, lnp_ref[...])


def _merge(x, u, b_out, m_out, l, ln, wg, bg, mix, ps, wbr, wo, lnp):
    B, S, _ = x.shape
    tm = TM_MERGE
    hb = tm // HALO
    n_halo = S // HALO
    row = lambda b, i: (b, i, 0)
    t = np.arange(S)
    icnt = np.stack([1.0 / (np.minimum(t + w // 2, S) - np.maximum(t - w // 2, 0)) for w in POOL_WINDOWS], axis=1)
    return pl.pallas_call(
        _merge_kernel,
        grid=(B, S // tm),
        in_specs=[
            pl.BlockSpec((1, tm, D_MODEL), row),
            pl.BlockSpec((1, tm, POOL_WIDTH), row),
            pl.BlockSpec((1, HALO, POOL_WIDTH), lambda b, i: (b, jnp.maximum(i * hb - 1, 0), 0)),
            pl.BlockSpec((1, HALO, POOL_WIDTH), lambda b, i: (b, jnp.minimum((i + 1) * hb, n_halo - 1), 0)),
            pl.BlockSpec((tm, POOL_GROUPS), lambda b, i: (i, 0)),
            pl.BlockSpec((1, tm, MLA_WIDTH), row),
            pl.BlockSpec((1, tm, X_WIDTH), row),
            _const_spec((1, D_MODEL), l),
            _const_spec((D_MODEL, N_BRANCH * D_MODEL), l),
            _const_spec((1, N_BRANCH * D_MODEL), l),
            _const_spec((POOL_GROUPS, POOL_GROUP_W, POOL_GROUP_W), l),
            _const_spec((1, POOL_WIDTH), l),
            _const_spec((N_BRANCH, BRANCH_WIDTH, D_MODEL), l),
            _const_spec((D_MODEL, D_MODEL), l),
            _const_spec((1, D_MODEL), l),
        ],
        out_specs=pl.BlockSpec((1, tm, D_MODEL), row),
        out_shape=jax.ShapeDtypeStruct((B, S, D_MODEL), F32),
        scratch_shapes=[
            pltpu.VMEM((tm + 3 * HALO, POOL_WIDTH), F32),
            pltpu.VMEM((2, tm + 3 * HALO - 8, POOL_GROUP_W), F32),
        ],
        name="merge",
        compiler_params=_params(),
    )(x, u, u, u, jnp.asarray(icnt, F32), b_out, m_out, ln, wg, bg, mix, ps, wbr, wo, lnp)


def _ffn_kernel(x_ref, ln_ref, wgu_ref, wd_ref, lnp_ref, o_ref):
    half = x_ref.shape[0] // 2
    for r in range(2):
        rows = slice(r * half, (r + 1) * half)
        x = x_ref[rows, :]
        h = _rms(x, ln_ref[...]).astype(BF)
        f = None
        for c in range(D_FF // FF_CHUNK):
            g = jnp.dot(h, wgu_ref[:, c * FF_CHUNK:(c + 1) * FF_CHUNK], preferred_element_type=F32)
            u = jnp.dot(h, wgu_ref[:, D_FF + c * FF_CHUNK:D_FF + (c + 1) * FF_CHUNK], preferred_element_type=F32)
            a = (g * jax.nn.sigmoid(g) * u).astype(BF)
            part = jnp.dot(a, wd_ref[c * FF_CHUNK:(c + 1) * FF_CHUNK, :], preferred_element_type=F32)
            f = part if f is None else f + part
        o_ref[rows, :] = x + _rms(f, lnp_ref[...])


def _ffn(x, l, ln, wgu, wd, lnp):
    B, S, _ = x.shape
    n = B * S
    tm = TM_FFN
    out = pl.pallas_call(
        _ffn_kernel,
        grid=(n // tm,),
        in_specs=[
            pl.BlockSpec((tm, D_MODEL), lambda i: (i, 0)),
            _const_spec((1, D_MODEL), l),
            _const_spec((D_MODEL, 2 * D_FF), l),
            _const_spec((D_FF, D_MODEL), l),
            _const_spec((1, D_MODEL), l),
        ],
        out_specs=pl.BlockSpec((tm, D_MODEL), lambda i: (i, 0)),
        out_shape=jax.ShapeDtypeStruct((n, D_MODEL), F32),
        name="ffn",
        compiler_params=_params(),
    )(x.reshape(n, D_MODEL), ln, wgu, wd, lnp)
    return out.reshape(B, S, D_MODEL)


def _rope_tables(seq):
    inv = 1.0 / (ROPE_THETA ** (np.arange(0, QK_ROPE, 2, dtype=np.float32) / QK_ROPE))
    ang = np.arange(seq, dtype=np.float32)[:, None] * inv[None, :]
    cos, sin = np.cos(ang), np.sin(ang)
    ct = np.zeros((seq, LANES), np.float32)
    sa = np.zeros((seq, LANES), np.float32)
    sb = np.zeros((seq, LANES), np.float32)
    x1 = slice(QK_NOPE, QK_NOPE + ROPE_HALF)
    x2 = slice(QK_NOPE + ROPE_HALF, QK_NOPE + QK_ROPE)
    ct[:, :QK_NOPE] = 1.0
    ct[:, x1] = cos
    ct[:, x2] = cos
    sa[:, x2] = sin
    sb[:, x1] = -sin
    return jnp.asarray(ct), jnp.asarray(sa), jnp.asarray(sb)


def _head_slabs(w, width, offset):
    L, K, _ = w.shape
    w = w.reshape(L, K, MLA_HEADS, width)
    w = jnp.pad(w, ((0, 0), (0, 0), (0, 0), (offset, HEAD_PAD - width - offset)))
    return w.reshape(L, K, MLA_HEADS * HEAD_PAD)


def _prep_weights(w_in, w_uq, w_uk, w_uv):
    pool, cq, ckv, kr, qx = (w_in[..., a:b] for a, b in
                             ((0, 512), (512, 896), (896, 1152), (1152, 1184), (1184, 1696)))
    kr_slab = jnp.pad(kr, ((0, 0), (0, 0), (QK_NOPE, LANES - QK_NOPE - QK_ROPE)))
    w_in_p = jnp.concatenate([pool, cq, ckv, qx, kr_slab], axis=-1).astype(BF)
    w_uq_p = _head_slabs(w_uq, QK_NOPE + QK_ROPE, 0).astype(BF)
    w_uk_p = _head_slabs(w_uk, QK_NOPE, 0).astype(BF)
    v_even = _head_slabs(w_uv, V_HEAD, 0).reshape(DEPTH, KV_LORA, MLA_HEADS, HEAD_PAD)
    v_odd = _head_slabs(w_uv, V_HEAD, V_HEAD).reshape(DEPTH, KV_LORA, MLA_HEADS, HEAD_PAD)
    odd = (jnp.arange(MLA_HEADS) % 2 == 1)[None, None, :, None]
    w_uv_p = jnp.where(odd, v_odd, v_even).reshape(DEPTH, KV_LORA, MLA_HEADS * HEAD_PAD).astype(BF)
    vone = np.zeros((MLA_HEADS, HEAD_PAD), np.float32)
    vone[0::2, V_HEAD] = 1.0
    vone[1::2, 0] = 1.0
    return w_in_p, w_uq_p, w_uk_p, w_uv_p, jnp.asarray(vone.reshape(1, -1))


def kernel(x_prompt, x_sample, mem_prompt, mem_sample, w_in, q_norm, kv_norm, w_uq, w_uk, w_uv, pool_mix,
           pool_scale, mem_norm, w_mem_kv, w_branch, w_gate, b_gate, w_out, ln_mix_pre, ln_mix_post,
           ln_ffn_pre, ln_ffn_post, w_gu, w_down):
    w_in_p, w_uq_p, w_uk_p, w_uv_p, vone = _prep_weights(w_in, w_uq, w_uk, w_uv)
    w_mem_kv_b = w_mem_kv.astype(BF)
    w_gate_b, w_branch_b, w_out_b = w_gate.astype(BF), w_branch.astype(BF), w_out.astype(BF)
    pool_mix_b, w_gu_b, w_down_b = pool_mix.astype(BF), w_gu.astype(BF), w_down.astype(BF)
    vec = lambda a: a[:, None, :]
    ln_mix_pre, ln_mix_post, ln_ffn_pre, ln_ffn_post = map(vec, (ln_mix_pre, ln_mix_post, ln_ffn_pre, ln_ffn_post))
    q_norm, kv_norm, b_gate, pool_scale = map(vec, (q_norm, kv_norm, b_gate, pool_scale))

    def trunk(x, mem):
        tabs = _rope_tables(x.shape[1])
        kmem, vmem = _memkv(mem, mem_norm, w_mem_kv_b)
        for l in range(DEPTH):
            u, q, k, v, m_out = _pre(x, tabs, kmem, vmem, l, ln_mix_pre, w_in_p, q_norm, kv_norm,
                                     w_uq_p, w_uk_p, w_uv_p, vone)
            b_out = _attn(q, k, v)
            x = _merge(x, u, b_out, m_out, l, ln_mix_pre, w_gate_b, b_gate, pool_mix_b, pool_scale,
                       w_branch_b, w_out_b, ln_mix_post)
            x = _ffn(x, l, ln_ffn_pre, w_gu_b, w_down_b, ln_ffn_post)
        return x

    return (trunk(x_prompt, mem_prompt), trunk(x_sample, mem_sample))
```
